```python
import jax, jax.numpy as jnp
from jax import lax
import numpy as np

D_MODEL = 2048
BATCH = 1
SEQ = 8192
DEPTH = 2

CHUNK = 64
N_MIXERS = 2
N_HGRN_LAYERS = (DEPTH + 1) // 2
N_GMLP_LAYERS = DEPTH // 2

HGRN_EXPAND = 128
HGRN_HEADS = D_MODEL // HGRN_EXPAND
FORGET_DIM = HGRN_HEADS * HGRN_EXPAND
HGRN_HEAD_V = D_MODEL // HGRN_HEADS
HGRN_IN_WIDTH = 2 * FORGET_DIM + 2 * D_MODEL

GMLP_BLOCK = 128
GMLP_HALF = D_MODEL
GMLP_GROUPS = 16
GMLP_GROUP_DIM = GMLP_HALF // GMLP_GROUPS

D_FF = 4 * D_MODEL
NORM_EPS = 1e-6

kernel_name = "hgrn2_gmlp_interleaved_trunk"


def rms_norm(x, gain):
    xf = x.astype(jnp.float32)
    y = xf * lax.rsqrt(jnp.mean(xf * xf, axis=-1, keepdims=True) + NORM_EPS)
    return (y * gain.astype(jnp.float32)).astype(x.dtype)


def layer_norm(x, gain, bias):
    xf = x.astype(jnp.float32)
    mu = jnp.mean(xf, axis=-1, keepdims=True)
    xc = xf - mu
    y = xc * lax.rsqrt(jnp.mean(xc * xc, axis=-1, keepdims=True) + NORM_EPS)
    return (y * gain.astype(jnp.float32) + bias.astype(jnp.float32)).astype(x.dtype)


def hgrn2_mixer(h, w_in, w_out, g_norm, lb):
    B, S, _ = h.shape
    proj = h @ w_in
    q, f, i, g = jnp.split(proj, [FORGET_DIM, 2 * FORGET_DIM, 2 * FORGET_DIM + D_MODEL], axis=-1)
    q = jax.nn.silu(q.astype(jnp.float32))
    forget = lb + (1.0 - lb) * jax.nn.sigmoid(f.astype(jnp.float32))
    k = 1.0 - forget
    log_f = jnp.log(forget)
    v = i.astype(jnp.float32)
    nc = S // CHUNK

    def to_chunks(t, d):
        return t.reshape(B, nc, CHUNK, HGRN_HEADS, d).transpose(1, 0, 3, 2, 4)

    qc = to_chunks(q, HGRN_EXPAND)
    kc = to_chunks(k, HGRN_EXPAND)
    gc = to_chunks(log_f, HGRN_EXPAND)
    vc = to_chunks(v, HGRN_HEAD_V)
    causal = jnp.tril(jnp.ones((CHUNK, CHUNK), dtype=bool))

    def step(state, inp):
        q_t, k_t, g_t, v_t = inp
        b = jnp.cumsum(g_t, axis=2)
        o_inter = jnp.einsum('bhtd,bhde->bhte', q_t * jnp.exp(b), state)
        rel = b[:, :, :, None, :] - b[:, :, None, :, :]
        decay = jnp.exp(jnp.where(causal[:, :, None], rel, -jnp.inf))
        scores = jnp.einsum('bhtd,bhtsd,bhsd->bhts', q_t, decay, k_t)
        o = o_inter + jnp.einsum('bhts,bhse->bhte', scores, v_t)
        b_last = b[:, :, -1:, :]
        new_state = (jnp.exp(b_last[:, :, 0, :])[..., None] * state
                     + jnp.einsum('bhsd,bhse->bhde', k_t * jnp.exp(b_last - b), v_t))
        return new_state, o

    state0 = jnp.zeros((B, HGRN_HEADS, HGRN_EXPAND, HGRN_HEAD_V), jnp.float32)
    _, o = lax.scan(step, state0, (qc, kc, gc, vc))
    o = o.transpose(1, 0, 3, 2, 4).reshape(B, S, HGRN_HEADS, HGRN_HEAD_V)
    o = o * lax.rsqrt(jnp.mean(o * o, axis=-1, keepdims=True) + NORM_EPS) * g_norm.astype(jnp.float32)
    gate = jax.nn.silu(g.astype(jnp.float32)).reshape(B, S, HGRN_HEADS, HGRN_HEAD_V)
    o = (o * gate).reshape(B, S, D_MODEL).astype(h.dtype)
    return o @ w_out


def gmlp_mixer(h, w_in, w_out, ln_gain, ln_bias, w_spatial, b_spatial):
    B, S, _ = h.shape
    z = jax.nn.gelu(h @ w_in, approximate=False)
    u, v = jnp.split(z, 2, axis=-1)
    v = layer_norm(v, ln_gain, ln_bias)
    nb = S // GMLP_BLOCK
    v = v.reshape(B, nb, GMLP_BLOCK, GMLP_GROUPS, GMLP_GROUP_DIM)
    chunk_id = jnp.arange(GMLP_BLOCK) // CHUNK
    mask = chunk_id[None, :] <= chunk_id[:, None]
    w = jnp.where(mask[None], w_spatial, 0)
    mixed = jnp.einsum('gts,bnsgc->bntgc', w, v) + b_spatial.T[None, None, :, :, None]
    out = u * mixed.reshape(B, S, GMLP_HALF)
    return out @ w_out


def sq_relu_mlp(h, w1, w2):
    a = jax.nn.relu(h @ w1)
    return (a * a) @ w2


def setup_inputs(seed: int = 0) -> dict:
    key = jax.random.key(seed)
    ks = jax.random.split(key, 16)
    f32 = jnp.float32
    nrm = lambda k, shape, scale: jax.random.normal(k, shape, f32) * scale
    return {
        "x": nrm(ks[0], (BATCH, SEQ, D_MODEL), 1.0),
        "norm_mix": 1.0 + nrm(ks[1], (DEPTH, D_MODEL), 0.05),
        "norm_mlp": 1.0 + nrm(ks[2], (DEPTH, D_MODEL), 0.05),
        "final_norm": 1.0 + nrm(ks[3], (D_MODEL,), 0.05),
        "hgrn_w_in": nrm(ks[4], (N_HGRN_LAYERS, D_MODEL, HGRN_IN_WIDTH), D_MODEL ** -0.5),
        "hgrn_w_out": nrm(ks[5], (N_HGRN_LAYERS, D_MODEL, D_MODEL), D_MODEL ** -0.5),
        "hgrn_g_norm": 1.0 + nrm(ks[6], (N_HGRN_LAYERS, HGRN_HEAD_V), 0.05),
        "hgrn_lb_logits": nrm(ks[7], (DEPTH + 1, FORGET_DIM), 0.5),
        "gmlp_w_in": nrm(ks[8], (N_GMLP_LAYERS, D_MODEL, 2 * GMLP_HALF), D_MODEL ** -0.5),
        "gmlp_w_out": nrm(ks[9], (N_GMLP_LAYERS, GMLP_HALF, D_MODEL), GMLP_HALF ** -0.5),
        "gmlp_ln_gain": 1.0 + nrm(ks[10], (N_GMLP_LAYERS, GMLP_HALF), 0.05),
        "gmlp_ln_bias": nrm(ks[11], (N_GMLP_LAYERS, GMLP_HALF), 0.02),
        "gmlp_w_spatial": nrm(ks[12], (N_GMLP_LAYERS, GMLP_GROUPS, GMLP_BLOCK, GMLP_BLOCK), GMLP_BLOCK ** -0.5),
        "gmlp_b_spatial": 1.0 + nrm(ks[13], (N_GMLP_LAYERS, GMLP_GROUPS, GMLP_BLOCK), 0.1),
        "mlp_w1": nrm(ks[14], (DEPTH, D_MODEL, D_FF), D_MODEL ** -0.5),
        "mlp_w2": nrm(ks[15], (DEPTH, D_FF, D_MODEL), D_FF ** -0.5),
    }


def reference(x, norm_mix, norm_mlp, final_norm, hgrn_w_in, hgrn_w_out, hgrn_g_norm,
              hgrn_lb_logits, gmlp_w_in, gmlp_w_out, gmlp_ln_gain, gmlp_ln_bias,
              gmlp_w_spatial, gmlp_b_spatial, mlp_w1, mlp_w2):
    lb_table = jnp.cumsum(jax.nn.softmax(hgrn_lb_logits.astype(jnp.float32), axis=0), axis=0)
    h = x
    for i in range(DEPTH):
        j = i // N_MIXERS
        hn = rms_norm(h, norm_mix[i])
        if i % N_MIXERS == 0:
            mix = hgrn2_mixer(hn, hgrn_w_in[j], hgrn_w_out[j], hgrn_g_norm[j], lb_table[i])
        else:
            mix = gmlp_mixer(hn, gmlp_w_in[j], gmlp_w_out[j], gmlp_ln_gain[j], gmlp_ln_bias[j],
                             gmlp_w_spatial[j], gmlp_b_spatial[j])
        h = h + mix.astype(h.dtype)
        h = h + sq_relu_mlp(rms_norm(h, norm_mlp[i]), mlp_w1[i], mlp_w2[i]).astype(h.dtype)
    return rms_norm(h, final_norm)
```

```python
import functools
import math

import jax
import jax.numpy as jnp
from jax import lax
from jax.experimental import pallas as pl
from jax.experimental.pallas import tpu as pltpu

F32 = jnp.float32
BF16 = jnp.bfloat16

NORM_EPS = 1e-6
CHUNK = 64
SUB = 16
HEAD = 128
GMLP_BLOCK = 128
N_GROUPS = 16

VMEM_LIMIT_BYTES = 56 * 1024 * 1024


def _dot(a, b):
    return jnp.dot(a, b, preferred_element_type=F32)


def _dot_nt(a, b):
    return lax.dot_general(a, b, (((1,), (1,)), ((), ())), preferred_element_type=F32)


def _sigmoid(x):
    return 1.0 / (1.0 + jnp.exp(-x))


def _rms_scale(h, gain):
    ms = jnp.mean(h * h, axis=-1, keepdims=True)
    return h * lax.rsqrt(ms + NORM_EPS) * gain


def _rmsnorm_kernel(x_ref, g_ref, o_ref):
    o_ref[...] = _rms_scale(x_ref[...], g_ref[...]).astype(o_ref.dtype)


def _rmsnorm(x, gain, *, tm=512):
    s, d = x.shape
    return pl.pallas_call(
        _rmsnorm_kernel,
        out_shape=jax.ShapeDtypeStruct((s, d), BF16),
        grid=(s // tm,),
        in_specs=[pl.BlockSpec((tm, d), lambda m: (m, 0)),
                  pl.BlockSpec((1, d), lambda m: (0, 0))],
        out_specs=pl.BlockSpec((tm, d), lambda m: (m, 0)),
        compiler_params=pltpu.CompilerParams(
            dimension_semantics=("arbitrary",), vmem_limit_bytes=VMEM_LIMIT_BYTES),
        name="rmsnorm",
    )(x, gain.reshape(1, d))


def _split3_bf16(x):
    hi = x.astype(BF16)
    r1 = x - hi.astype(F32)
    mid = r1.astype(BF16)
    lo = (r1 - mid.astype(F32)).astype(BF16)
    return hi, mid, lo


def _hgrn_kernel(layer_idx, n_chunks, hn_ref, w_ref, lbl_ref, gn_ref, o_ref, st_ref):
    m = pl.program_id(1)

    @pl.when(m == 0)
    def _():
        st_ref[...] = jnp.zeros_like(st_ref)

    lg = lbl_ref[...]
    lmax = jnp.max(lg, axis=0, keepdims=True)
    le = jnp.exp(lg - lmax)
    lb = jnp.sum(le[0:layer_idx + 1], axis=0, keepdims=True) / jnp.sum(le, axis=0, keepdims=True)

    proj = _dot(hn_ref[...], w_ref[...])
    pq = proj[:, 0:HEAD]
    pf = proj[:, HEAD:2 * HEAD]
    v_all = proj[:, 2 * HEAD:3 * HEAD]
    pg = proj[:, 3 * HEAD:4 * HEAD]
    q_all = pq * _sigmoid(pq)
    forget = lb + (1.0 - lb) * _sigmoid(pf)
    k_all = 1.0 - forget
    logf_all = jnp.log(forget)
    gate_all = pg * _sigmoid(pg) * gn_ref[...]

    row = lax.broadcasted_iota(jnp.int32, (CHUNK, CHUNK), 0)
    col = lax.broadcasted_iota(jnp.int32, (CHUNK, CHUNK), 1)
    tril = (col <= row).astype(BF16)
    mask_sub = (row // SUB == col // SUB) & (col <= row)
    level_masks = []
    half = SUB
    while half < CHUNK:
        blk = 2 * half
        level_masks.append((half, (row // blk == col // blk) & (row % blk >= half) & (col % blk < half)))
        half = blk

    st = st_ref[...]
    for c in range(n_chunks):
        sl = slice(c * CHUNK, (c + 1) * CHUNK)
        q, k, v, logf = q_all[sl], k_all[sl], v_all[sl], logf_all[sl]
        hi, mid, lo = _split3_bf16(logf)
        cs = _dot(tril, jnp.concatenate([hi, mid, lo], axis=1))
        b = cs[:, 0:HEAD] + cs[:, HEAD:2 * HEAD] + cs[:, 2 * HEAD:3 * HEAD]
        b_last = b[CHUNK - 1:CHUNK, :]

        st_bf = st.astype(BF16)
        o = _dot_nt((q * jnp.exp(b)).astype(BF16), st_bf)

        e_parts = []
        for i in range(CHUNK // SUB):
            bi = b[i * SUB:(i + 1) * SUB]
            base = b[i * SUB - 1:i * SUB, :] if i > 0 else jnp.zeros_like(b_last)
            e_parts.append(bi - base)
        e_sub = jnp.concatenate(e_parts, axis=0)
        scores = jnp.where(
            mask_sub,
            _dot_nt((q * jnp.exp(e_sub)).astype(BF16), (k * jnp.exp(-e_sub)).astype(BF16)),
            0.0)
        for half, lmask in level_masks:
            blk = 2 * half
            z_parts = []
            for j in range(CHUNK // blk):
                ref_row = j * blk + half - 1
                z_parts.append(b[j * blk:(j + 1) * blk] - b[ref_row:ref_row + 1, :])
            z = jnp.exp(-jnp.abs(jnp.concatenate(z_parts, axis=0)))
            scores = scores + jnp.where(
                lmask, _dot_nt((q * z).astype(BF16), (k * z).astype(BF16)), 0.0)
        o = o + _dot(scores.astype(BF16), v.astype(BF16))

        k_hat = (k * jnp.exp(b_last - b)).astype(BF16)
        st = st * jnp.exp(b_last) + _dot(v.T.astype(BF16), k_hat)

        ms = jnp.mean(o * o, axis=-1, keepdims=True)
        o_ref[sl, :] = (o * lax.rsqrt(ms + NORM_EPS) * gate_all[sl]).astype(o_ref.dtype)
    st_ref[...] = st


def _hgrn_mixer(hn, w_heads, lb_logits, g_norm, *, layer_idx, tm=512):
    s, d = hn.shape
    n_heads = w_heads.shape[0]
    n_lb = lb_logits.shape[0]
    kern = functools.partial(_hgrn_kernel, layer_idx, tm // CHUNK)
    return pl.pallas_call(
        kern,
        out_shape=jax.ShapeDtypeStruct((s, n_heads * HEAD), BF16),
        grid=(n_heads, s // tm),
        in_specs=[pl.BlockSpec((tm, d), lambda h, m: (m, 0)),
                  pl.BlockSpec((None, d, 4 * HEAD), lambda h, m: (h, 0, 0)),
                  pl.BlockSpec((n_lb, HEAD), lambda h, m: (0, h)),
                  pl.BlockSpec((1, HEAD), lambda h, m: (0, 0))],
        out_specs=pl.BlockSpec((tm, HEAD), lambda h, m: (m, h)),
        scratch_shapes=[pltpu.VMEM((HEAD, HEAD), F32)],
        compiler_params=pltpu.CompilerParams(
            dimension_semantics=("arbitrary", "arbitrary"), vmem_limit_bytes=VMEM_LIMIT_BYTES),
        name="hgrn_mixer",
    )(hn, w_heads, lb_logits, g_norm.reshape(1, HEAD))


def _mm_res_norm_kernel(n_k, emit_h, a_ref, w_ref, r_ref, g_ref, *out_refs):
    if emit_h:
        h_ref, hn_ref = out_refs
        acc_ref = h_ref
    else:
        hn_ref, acc_ref = out_refs
    kk = pl.program_id(1)

    @pl.when(kk == 0)
    def _():
        acc_ref[...] = r_ref[...]

    acc_ref[...] += _dot(a_ref[...], w_ref[...])

    @pl.when(kk == n_k - 1)
    def _():
        hn_ref[...] = _rms_scale(acc_ref[...], g_ref[...]).astype(hn_ref.dtype)


def _mm_res_norm(a, w, resid, gain, *, emit_h, tm=512, tk=1024):
    s, kdim = a.shape
    d = w.shape[1]
    tk = min(tk, kdim)
    n_k = kdim // tk
    kern = functools.partial(_mm_res_norm_kernel, n_k, emit_h)
    row_spec = pl.BlockSpec((tm, d), lambda m, k: (m, 0))
    if emit_h:
        out_shape = (jax.ShapeDtypeStruct((s, d), F32), jax.ShapeDtypeStruct((s, d), BF16))
        out_specs = (row_spec, row_spec)
        scratch = []
    else:
        out_shape = jax.ShapeDtypeStruct((s, d), F32)
        out_specs = row_spec
        scratch = [pltpu.VMEM((tm, d), F32)]
    return pl.pallas_call(
        kern,
        out_shape=out_shape,
        grid=(s // tm, n_k),
        in_specs=[pl.BlockSpec((tm, tk), lambda m, k: (m, k)),
                  pl.BlockSpec((tk, d), lambda m, k: (k, 0)),
                  row_spec,
                  pl.BlockSpec((1, d), lambda m, k: (0, 0))],
        out_specs=out_specs,
        scratch_shapes=scratch,
        compiler_params=pltpu.CompilerParams(
            dimension_semantics=("arbitrary", "arbitrary"), vmem_limit_bytes=VMEM_LIMIT_BYTES),
        name="matmul_residual_rmsnorm",
    )(a, w, resid, gain.reshape(1, d))


def _mlp_up_kernel(x_ref, w_ref, o_ref):
    a = jnp.maximum(_dot(x_ref[...], w_ref[...]), 0.0)
    o_ref[...] = (a * a).astype(o_ref.dtype)


def _mlp_up(x, w, *, tm=512, tn=2048):
    s, d = x.shape
    f = w.shape[1]
    return pl.pallas_call(
        _mlp_up_kernel,
        out_shape=jax.ShapeDtypeStruct((s, f), BF16),
        grid=(f // tn, s // tm),
        in_specs=[pl.BlockSpec((tm, d), lambda n, m: (m, 0)),
                  pl.BlockSpec((d, tn), lambda n, m: (0, n))],
        out_specs=pl.BlockSpec((tm, tn), lambda n, m: (m, n)),
        compiler_params=pltpu.CompilerParams(
            dimension_semantics=("arbitrary", "arbitrary"), vmem_limit_bytes=VMEM_LIMIT_BYTES),
        name="mlp_up_relu2",
    )(x, w)


def _gmlp_kernel(n_blocks, hn_ref, w_ref, lng_ref, lnb_ref, wsp_ref, bsp_ref, o_ref):
    half = lng_ref.shape[1]
    z = _dot(hn_ref[...], w_ref[...])
    z = 0.5 * z * (1.0 + lax.erf(z * (1.0 / math.sqrt(2.0))))
    u = z[:, 0:half]
    v = z[:, half:2 * half]
    mu = jnp.mean(v, axis=-1, keepdims=True)
    vc = v - mu
    var = jnp.mean(vc * vc, axis=-1, keepdims=True)
    vn = (vc * lax.rsqrt(var + NORM_EPS) * lng_ref[...] + lnb_ref[...]).astype(BF16)

    row = lax.broadcasted_iota(jnp.int32, (GMLP_BLOCK, GMLP_BLOCK), 0)
    col = lax.broadcasted_iota(jnp.int32, (GMLP_BLOCK, GMLP_BLOCK), 1)
    causal = (col // CHUNK) <= (row // CHUNK)
    gd = half // N_GROUPS
    for g in range(N_GROUPS):
        wg = jnp.where(causal, wsp_ref[g], 0.0).astype(BF16)
        bias = bsp_ref[:, g:g + 1]
        cs = slice(g * gd, (g + 1) * gd)
        for r in range(n_blocks):
            rs = slice(r * GMLP_BLOCK, (r + 1) * GMLP_BLOCK)
            mixed = _dot(wg, vn[rs, cs]) + bias
            o_ref[rs, cs] = (u[rs, cs] * mixed).astype(o_ref.dtype)


def _gmlp_mixer(hn, w_in, ln_gain, ln_bias, w_spatial, b_spatial_t, *, tm=256):
    s, d = hn.shape
    two_half = w_in.shape[1]
    half = two_half // 2
    kern = functools.partial(_gmlp_kernel, tm // GMLP_BLOCK)
    const2 = lambda m: (0, 0)
    return pl.pallas_call(
        kern,
        out_shape=jax.ShapeDtypeStruct((s, half), BF16),
        grid=(s // tm,),
        in_specs=[pl.BlockSpec((tm, d), lambda m: (m, 0)),
                  pl.BlockSpec((d, two_half), const2, pipeline_mode=pl.Buffered(1)),
                  pl.BlockSpec((1, half), const2),
                  pl.BlockSpec((1, half), const2),
                  pl.BlockSpec((N_GROUPS, GMLP_BLOCK, GMLP_BLOCK), lambda m: (0, 0, 0)),
                  pl.BlockSpec((GMLP_BLOCK, N_GROUPS), const2)],
        out_specs=pl.BlockSpec((tm, half), lambda m: (m, 0)),
        compiler_params=pltpu.CompilerParams(
            dimension_semantics=("arbitrary",), vmem_limit_bytes=VMEM_LIMIT_BYTES),
        name="gmlp_mixer",
    )(hn, w_in, ln_gain.reshape(1, half), ln_bias.reshape(1, half), w_spatial, b_spatial_t)


def kernel(x, norm_mix, norm_mlp, final_norm, hgrn_w_in, hgrn_w_out, hgrn_g_norm, hgrn_lb_logits,
           gmlp_w_in, gmlp_w_out, gmlp_ln_gain, gmlp_ln_bias, gmlp_w_spatial, gmlp_b_spatial,
           mlp_w1, mlp_w2):
    b, s, d = x.shape
    depth = norm_mix.shape[0]
    n_heads = d // HEAD
    h = x.reshape(b * s, d)
    assert b == 1, "sequence mixing assumes a single sequence"

    hn = _rmsnorm(h, norm_mix[0])
    out = None
    for i in range(depth):
        j = i // 2
        if i % 2 == 0:
            w_heads = (hgrn_w_in[j].astype(BF16)
                       .reshape(d, 4, n_heads, HEAD).transpose(2, 0, 1, 3).reshape(n_heads, d, 4 * HEAD))
            mix = _hgrn_mixer(hn, w_heads, hgrn_lb_logits, hgrn_g_norm[j], layer_idx=i)
            w_out = hgrn_w_out[j]
        else:
            mix = _gmlp_mixer(hn, gmlp_w_in[j].astype(BF16), gmlp_ln_gain[j], gmlp_ln_bias[j],
                              gmlp_w_spatial[j], gmlp_b_spatial[j].T)
            w_out = gmlp_w_out[j]
        h, hn = _mm_res_norm(mix, w_out.astype(BF16), h, norm_mlp[i], emit_h=True)
        a = _mlp_up(hn, mlp_w1[i].astype(BF16))
        if i + 1 < depth:
            h, hn = _mm_res_norm(a, mlp_w2[i].astype(BF16), h, norm_mix[i + 1], emit_h=True)
        else:
            out = _mm_res_norm(a, mlp_w2[i].astype(BF16), h, final_norm, emit_h=False)
    return out.reshape(b, s, d)
```

```python
import functools
import math

import jax
import jax.numpy as jnp
from jax import lax
from jax.experimental import pallas as pl
from jax.experimental.pallas import tpu as pltpu

F32 = jnp.float32
BF16 = jnp.bfloat16

NORM_EPS = 1e-6
CHUNK = 64
SUB = 16
HEAD = 128
GMLP_BLOCK = 128
N_GROUPS = 16

VMEM_LIMIT_BYTES = 56 * 1024 * 1024


def _dot(a, b):
    return jnp.dot(a, b, preferred_element_type=F32)


def _dot_nt(a, b):
    return lax.dot_general(a, b, (((1,), (1,)), ((), ())), preferred_element_type=F32)


def _sigmoid(x):
    return 1.0 / (1.0 + jnp.exp(-x))


def _rms_scale(h, gain):
    ms = jnp.mean(h * h, axis=-1, keepdims=True)
    return h * lax.rsqrt(ms + NORM_EPS) * gain


def _params(n_axes):
    return pltpu.CompilerParams(
        dimension_semantics=("arbitrary",) * n_axes, vmem_limit_bytes=VMEM_LIMIT_BYTES)


def _rmsnorm_kernel(x_ref, g_ref, o_ref):
    o_ref[...] = _rms_scale(x_ref[...], g_ref[...]).astype(o_ref.dtype)


def _rmsnorm(x, gain, *, tm=512):
    s, d = x.shape
    return pl.pallas_call(
        _rmsnorm_kernel,
        out_shape=jax.ShapeDtypeStruct((s, d), BF16),
        grid=(s // tm,),
        in_specs=[pl.BlockSpec((tm, d), lambda m: (m, 0)),
                  pl.BlockSpec((1, d), lambda m: (0, 0))],
        out_specs=pl.BlockSpec((tm, d), lambda m: (m, 0)),
        compiler_params=_params(1),
        name="rmsnorm",
    )(x, gain.reshape(1, d))


def _split3_bf16(x):
    hi = x.astype(BF16)
    r1 = x - hi.astype(F32)
    mid = r1.astype(BF16)
    lo = (r1 - mid.astype(F32)).astype(BF16)
    return hi, mid, lo


def _row_blocks(src, blk, row_of_block):
    parts = []
    for j in range(src.shape[0] // blk):
        r = row_of_block(j)
        row = jnp.zeros_like(src[0:1, :]) if r is None else src[r:r + 1, :]
        parts.append(jnp.broadcast_to(row, (blk, src.shape[1])))
    return jnp.concatenate(parts, axis=0)


def _hgrn_kernel(layer_idx, n_chunks, hn_ref, wq_ref, wf_ref, wi_ref, wg_ref, lbl_ref, gn_ref, o_ref,
                 w_bf, st_ref):
    m = pl.program_id(1)

    @pl.when(m == 0)
    def _():
        st_ref[...] = jnp.zeros_like(st_ref)
        for p, r in enumerate((wq_ref, wf_ref, wi_ref, wg_ref)):
            w_bf[:, p * HEAD:(p + 1) * HEAD] = r[...].astype(BF16)

    lg = lbl_ref[...]
    lmax = jnp.max(lg, axis=0, keepdims=True)
    le = jnp.exp(lg - lmax)
    lb = jnp.sum(le[0:layer_idx + 1], axis=0, keepdims=True) / jnp.sum(le, axis=0, keepdims=True)

    proj = _dot(hn_ref[...], w_bf[...])
    pq = proj[:, 0:HEAD]
    pf = proj[:, HEAD:2 * HEAD]
    v = proj[:, 2 * HEAD:3 * HEAD]
    pg = proj[:, 3 * HEAD:4 * HEAD]
    q = pq * _sigmoid(pq)
    forget = lb + (1.0 - lb) * _sigmoid(pf)
    k = 1.0 - forget
    logf = jnp.log(forget)
    gate = pg * _sigmoid(pg) * gn_ref[...]

    row = lax.broadcasted_iota(jnp.int32, (CHUNK, CHUNK), 0)
    col = lax.broadcasted_iota(jnp.int32, (CHUNK, CHUNK), 1)
    tril = (col <= row).astype(BF16)

    def chunk(x, c):
        return x[c * CHUNK:(c + 1) * CHUNK]

    def chunks_to_lanes(x):
        return jnp.concatenate([chunk(x, c) for c in range(n_chunks)], axis=1)

    hi, mid, lo = _split3_bf16(logf)
    cs = _dot(tril, jnp.concatenate([chunks_to_lanes(hi), chunks_to_lanes(mid), chunks_to_lanes(lo)], axis=1))
    wl = n_chunks * HEAD
    bl = cs[:, 0:wl] + cs[:, wl:2 * wl] + cs[:, 2 * wl:3 * wl]
    b = jnp.concatenate([bl[:, c * HEAD:(c + 1) * HEAD] for c in range(n_chunks)], axis=0)

    sub_per_chunk = CHUNK // SUB
    e_sub = b - _row_blocks(b, SUB, lambda i: None if i % sub_per_chunk == 0 else i * SUB - 1)
    b_last = _row_blocks(b, CHUNK, lambda c: c * CHUNK + CHUNK - 1)
    q_parts = [(q * jnp.exp(e_sub)).astype(BF16)]
    k_parts = [(k * jnp.exp(-e_sub)).astype(BF16)]
    masks = [(row // SUB == col // SUB) & (col <= row)]
    half = SUB
    while half < CHUNK:
        blk = 2 * half
        z = jnp.exp(-jnp.abs(b - _row_blocks(b, blk, lambda j, blk=blk, half=half: j * blk + half - 1)))
        q_parts.append((q * z).astype(BF16))
        k_parts.append((k * z).astype(BF16))
        masks.append((row // blk == col // blk) & (row % blk >= half) & (col % blk < half))
        half = blk
    q_b = (q * jnp.exp(b)).astype(BF16)
    k_hat = (k * jnp.exp(b_last - b)).astype(BF16)
    v_bf = v.astype(BF16)

    o_intra, upd, dec = [], [], []
    for c in range(n_chunks):
        scores = None
        for qp, kp, msk in zip(q_parts, k_parts, masks):
            s = jnp.where(msk, _dot_nt(chunk(qp, c), chunk(kp, c)), 0.0)
            scores = s if scores is None else scores + s
        o_intra.append(_dot(scores.astype(BF16), chunk(v_bf, c)))
        upd.append(_dot(chunk(v, c).T.astype(BF16), chunk(k_hat, c)))
        dec.append(jnp.exp(b[(c + 1) * CHUNK - 1:(c + 1) * CHUNK, :]))

    st = st_ref[...]
    outs = []
    for c in range(n_chunks):
        outs.append(_dot_nt(chunk(q_b, c), st.astype(BF16)) + o_intra[c])
        st = st * dec[c] + upd[c]
    st_ref[...] = st

    o = jnp.concatenate(outs, axis=0)
    ms = jnp.mean(o * o, axis=-1, keepdims=True)
    o_ref[...] = (o * lax.rsqrt(ms + NORM_EPS) * gate).astype(o_ref.dtype)


def _hgrn_mixer(hn, w_in, lb_logits, g_norm, *, w_layer, layer_idx, tm=512):
    s, d = hn.shape
    n_heads = w_in.shape[2] // (4 * HEAD)
    n_lb = lb_logits.shape[0]
    kern = functools.partial(_hgrn_kernel, layer_idx, tm // CHUNK)

    def w_spec(part):
        return pl.BlockSpec((None, d, HEAD), lambda h, m: (w_layer, 0, part * n_heads + h))

    return pl.pallas_call(
        kern,
        out_shape=jax.ShapeDtypeStruct((s, n_heads * HEAD), BF16),
        grid=(n_heads, s // tm),
        in_specs=[pl.BlockSpec((tm, d), lambda h, m: (m, 0)),
                  w_spec(0), w_spec(1), w_spec(2), w_spec(3),
                  pl.BlockSpec((n_lb, HEAD), lambda h, m: (0, h)),
                  pl.BlockSpec((1, HEAD), lambda h, m: (0, 0))],
        out_specs=pl.BlockSpec((tm, HEAD), lambda h, m: (m, h)),
        scratch_shapes=[pltpu.VMEM((d, 4 * HEAD), BF16), pltpu.VMEM((HEAD, HEAD), F32)],
        compiler_params=_params(2),
        name="hgrn_mixer",
    )(hn, w_in, w_in, w_in, w_in, lb_logits, g_norm.reshape(1, HEAD))


def _mm_res_norm_kernel(n_k, emit_h, a_ref, w_ref, r_ref, g_ref, *out_refs):
    if emit_h:
        h_ref, hn_ref = out_refs
        acc_ref = h_ref
    else:
        hn_ref, acc_ref = out_refs
    kk = pl.program_id(1)

    @pl.when(kk == 0)
    def _():
        acc_ref[...] = r_ref[...]

    acc_ref[...] += _dot(a_ref[...], w_ref[...])

    @pl.when(kk == n_k - 1)
    def _():
        hn_ref[...] = _rms_scale(acc_ref[...], g_ref[...]).astype(hn_ref.dtype)


def _mm_res_norm(a, w, resid, gain, *, w_layer, emit_h, tm=512, tk=1024):
    s, kdim = a.shape
    d = w.shape[2]
    tk = min(tk, kdim)
    n_k = kdim // tk
    kern = functools.partial(_mm_res_norm_kernel, n_k, emit_h)
    row_spec = pl.BlockSpec((tm, d), lambda m, k: (m, 0))
    if emit_h:
        out_shape = (jax.ShapeDtypeStruct((s, d), F32), jax.ShapeDtypeStruct((s, d), BF16))
        out_specs = (row_spec, row_spec)
        scratch = []
    else:
        out_shape = jax.ShapeDtypeStruct((s, d), F32)
        out_specs = row_spec
        scratch = [pltpu.VMEM((tm, d), F32)]
    return pl.pallas_call(
        kern,
        out_shape=out_shape,
        grid=(s // tm, n_k),
        in_specs=[pl.BlockSpec((tm, tk), lambda m, k: (m, k)),
                  pl.BlockSpec((None, tk, d), lambda m, k: (w_layer, k, 0)),
                  row_spec,
                  pl.BlockSpec((1, d), lambda m, k: (0, 0))],
        out_specs=out_specs,
        scratch_shapes=scratch,
        compiler_params=_params(2),
        name="matmul_residual_rmsnorm",
    )(a, w, resid, gain.reshape(1, d))


def _mlp_up_kernel(x_ref, w_ref, o_ref, w_bf):
    @pl.when(pl.program_id(1) == 0)
    def _():
        w_bf[...] = w_ref[...].astype(BF16)

    a = jnp.maximum(_dot(x_ref[...], w_bf[...]), 0.0)
    o_ref[...] = (a * a).astype(o_ref.dtype)


def _mlp_up(x, w, *, w_layer, tm=512, tn=1024):
    s, d = x.shape
    f = w.shape[2]
    return pl.pallas_call(
        _mlp_up_kernel,
        out_shape=jax.ShapeDtypeStruct((s, f), BF16),
        grid=(f // tn, s // tm),
        in_specs=[pl.BlockSpec((tm, d), lambda n, m: (m, 0)),
                  pl.BlockSpec((None, d, tn), lambda n, m: (w_layer, 0, n))],
        out_specs=pl.BlockSpec((tm, tn), lambda n, m: (m, n)),
        scratch_shapes=[pltpu.VMEM((d, tn), BF16)],
        compiler_params=_params(2),
        name="mlp_up_relu2",
    )(x, w)


def _gmlp_kernel(n_blocks, hn_ref, w_ref, lng_ref, lnb_ref, wsp_ref, bsp_ref, o_ref):
    half = lng_ref.shape[1]
    z = _dot(hn_ref[...], w_ref[...])
    z = 0.5 * z * (1.0 + lax.erf(z * (1.0 / math.sqrt(2.0))))
    u = z[:, 0:half]
    v = z[:, half:2 * half]
    mu = jnp.mean(v, axis=-1, keepdims=True)
    vc = v - mu
    var = jnp.mean(vc * vc, axis=-1, keepdims=True)
    vn = (vc * lax.rsqrt(var + NORM_EPS) * lng_ref[...] + lnb_ref[...]).astype(BF16)

    row = lax.broadcasted_iota(jnp.int32, (GMLP_BLOCK, GMLP_BLOCK), 0)
    col = lax.broadcasted_iota(jnp.int32, (GMLP_BLOCK, GMLP_BLOCK), 1)
    causal = (col // CHUNK) <= (row // CHUNK)
    gd = half // N_GROUPS
    for g in range(N_GROUPS):
        wg = jnp.where(causal, wsp_ref[g], 0.0).astype(BF16)
        bias = bsp_ref[:, g:g + 1]
        cs = slice(g * gd, (g + 1) * gd)
        for r in range(n_blocks):
            rs = slice(r * GMLP_BLOCK, (r + 1) * GMLP_BLOCK)
            mixed = _dot(wg, vn[rs, cs]) + bias
            o_ref[rs, cs] = (u[rs, cs] * mixed).astype(o_ref.dtype)


def _gmlp_mixer(hn, w_in, ln_gain, ln_bias, w_spatial, b_spatial_t, *, w_layer, tm=256):
    s, d = hn.shape
    two_half = w_in.shape[2]
    half = two_half // 2
    kern = functools.partial(_gmlp_kernel, tm // GMLP_BLOCK)
    const2 = lambda m: (0, 0)
    return pl.pallas_call(
        kern,
        out_shape=jax.ShapeDtypeStruct((s, half), BF16),
        grid=(s // tm,),
        in_specs=[pl.BlockSpec((tm, d), lambda m: (m, 0)),
                  pl.BlockSpec((None, d, two_half), lambda m: (w_layer, 0, 0), pipeline_mode=pl.Buffered(1)),
                  pl.BlockSpec((1, half), const2),
                  pl.BlockSpec((1, half), const2),
                  pl.BlockSpec((None, N_GROUPS, GMLP_BLOCK, GMLP_BLOCK), lambda m: (w_layer, 0, 0, 0)),
                  pl.BlockSpec((GMLP_BLOCK, N_GROUPS), const2)],
        out_specs=pl.BlockSpec((tm, half), lambda m: (m, 0)),
        compiler_params=_params(1),
        name="gmlp_mixer",
    )(hn, w_in, ln_gain.reshape(1, half), ln_bias.reshape(1, half), w_spatial, b_spatial_t)


def kernel(x, norm_mix, norm_mlp, final_norm, hgrn_w_in, hgrn_w_out, hgrn_g_norm, hgrn_lb_logits,
           gmlp_w_in, gmlp_w_out, gmlp_ln_gain, gmlp_ln_bias, gmlp_w_spatial, gmlp_b_spatial,
           mlp_w1, mlp_w2):
    b, s, d = x.shape
    depth = norm_mix.shape[0]
    h = x.reshape(b * s, d)
    assert b == 1, "sequence mixing assumes a single sequence"

    hgrn_w_out_bf = hgrn_w_out.astype(BF16)
    gmlp_w_in_bf = gmlp_w_in.astype(BF16)
    gmlp_w_out_bf = gmlp_w_out.astype(BF16)
    mlp_w2_bf = mlp_w2.astype(BF16)

    hn = _rmsnorm(h, norm_mix[0])
    out = None
    for i in range(depth):
        j = i // 2
        if i % 2 == 0:
            mix = _hgrn_mixer(hn, hgrn_w_in, hgrn_lb_logits, hgrn_g_norm[j], w_layer=j, layer_idx=i)
            w_out = hgrn_w_out_bf
        else:
            mix = _gmlp_mixer(hn, gmlp_w_in_bf, gmlp_ln_gain[j], gmlp_ln_bias[j],
                              gmlp_w_spatial, gmlp_b_spatial[j].T, w_layer=j)
            w_out = gmlp_w_out_bf
        h, hn = _mm_res_norm(mix, w_out, h, norm_mlp[i], w_layer=j, emit_h=True)
        a = _mlp_up(hn, mlp_w1, w_layer=i)
        if i + 1 < depth:
            h, hn = _mm_res_norm(a, mlp_w2_bf, h, norm_mix[i + 1], w_layer=i, emit_h=True)
        else:
            out = _mm_res_norm(a, mlp_w2_bf, h, final_norm, w_layer=i, emit_h=False)
    return out.reshape(b, s, d)
```

```python
import functools
import math

import jax
import jax.numpy as jnp
from jax import lax
from jax.experimental import pallas as pl
from jax.experimental.pallas import tpu as pltpu

F32 = jnp.float32
BF16 = jnp.bfloat16

NORM_EPS = 1e-6
CHUNK = 64
SUB = 16
TILE = 128
HEAD = 128
GMLP_BLOCK = 128
N_GROUPS = 16

VMEM_LIMIT_BYTES = 56 * 1024 * 1024


def _dot(a, b):
    return jnp.dot(a, b, preferred_element_type=F32)


def _dot_nt(a, b):
    return lax.dot_general(a, b, (((1,), (1,)), ((), ())), preferred_element_type=F32)


def _sigmoid(x):
    return 1.0 / (1.0 + jnp.exp(-x))


def _rms_scale(h, gain):
    ms = jnp.mean(h * h, axis=-1, keepdims=True)
    return h * lax.rsqrt(ms + NORM_EPS) * gain


def _params(n_axes):
    return pltpu.CompilerParams(
        dimension_semantics=("arbitrary",) * n_axes, vmem_limit_bytes=VMEM_LIMIT_BYTES)


def _rmsnorm_kernel(x_ref, g_ref, o_ref):
    o_ref[...] = _rms_scale(x_ref[...], g_ref[...]).astype(o_ref.dtype)


def _rmsnorm(x, gain, *, tm=512):
    s, d = x.shape
    return pl.pallas_call(
        _rmsnorm_kernel,
        out_shape=jax.ShapeDtypeStruct((s, d), BF16),
        grid=(s // tm,),
        in_specs=[pl.BlockSpec((tm, d), lambda m: (m, 0)),
                  pl.BlockSpec((1, d), lambda m: (0, 0))],
        out_specs=pl.BlockSpec((tm, d), lambda m: (m, 0)),
        compiler_params=_params(1),
        name="rmsnorm",
    )(x, gain.reshape(1, d))


def _split3_bf16(x):
    hi = x.astype(BF16)
    r1 = x - hi.astype(F32)
    mid = r1.astype(BF16)
    lo = (r1 - mid.astype(F32)).astype(BF16)
    return hi, mid, lo


def _row_blocks(src, blk, row_of_block):
    parts = []
    for j in range(src.shape[0] // blk):
        r = row_of_block(j)
        row = jnp.zeros_like(src[0:1, :]) if r is None else src[r:r + 1, :]
        parts.append(jnp.broadcast_to(row, (blk, src.shape[1])))
    return jnp.concatenate(parts, axis=0)


def _hgrn_kernel(layer_idx, n_chunks, hn_ref, wq_ref, wf_ref, wi_ref, wg_ref, lbl_ref, gn_ref, o_ref,
                 w_bf, st_ref):
    m = pl.program_id(1)

    @pl.when(m == 0)
    def _():
        st_ref[...] = jnp.zeros_like(st_ref)
        for p, r in enumerate((wq_ref, wf_ref, wi_ref, wg_ref)):
            w_bf[:, p * HEAD:(p + 1) * HEAD] = r[...].astype(BF16)

    lg = lbl_ref[...]
    lmax = jnp.max(lg, axis=0, keepdims=True)
    le = jnp.exp(lg - lmax)
    lb = jnp.sum(le[0:layer_idx + 1], axis=0, keepdims=True) / jnp.sum(le, axis=0, keepdims=True)

    proj = _dot(hn_ref[...], w_bf[...])
    pq = proj[:, 0:HEAD]
    pf = proj[:, HEAD:2 * HEAD]
    v = proj[:, 2 * HEAD:3 * HEAD]
    pg = proj[:, 3 * HEAD:4 * HEAD]
    q = pq * _sigmoid(pq)
    forget = lb + (1.0 - lb) * _sigmoid(pf)
    k = 1.0 - forget
    logf = jnp.log(forget)
    gate = pg * _sigmoid(pg) * gn_ref[...]

    tm = n_chunks * CHUNK
    n_tiles = tm // TILE

    def chunk(x, c):
        return x[c * CHUNK:(c + 1) * CHUNK]

    def chunks_to_lanes(x):
        return jnp.concatenate([chunk(x, c) for c in range(n_chunks)], axis=1)

    row_c = lax.broadcasted_iota(jnp.int32, (CHUNK, CHUNK), 0)
    col_c = lax.broadcasted_iota(jnp.int32, (CHUNK, CHUNK), 1)
    tril = (col_c <= row_c).astype(BF16)
    hi, mid, lo = _split3_bf16(logf)
    cs = _dot(tril, jnp.concatenate([chunks_to_lanes(hi), chunks_to_lanes(mid), chunks_to_lanes(lo)], axis=1))
    wl = n_chunks * HEAD
    bl = cs[:, 0:wl] + cs[:, wl:2 * wl] + cs[:, 2 * wl:3 * wl]
    b_parts = []
    offset = None
    for c in range(n_chunks):
        bc = bl[:, c * HEAD:(c + 1) * HEAD]
        if offset is not None:
            bc = bc + offset
        b_parts.append(bc)
        offset = bc[CHUNK - 1:CHUNK, :]
    b = jnp.concatenate(b_parts, axis=0)
    b_last = offset

    v_bf = v.astype(BF16)
    a_tiles = [[None] * n_tiles for _ in range(n_tiles)]

    half = tm // 2
    while half >= TILE:
        blk = 2 * half
        for j in range(tm // blk):
            ref_row = j * blk + half - 1
            t_rows = slice(j * blk + half, (j + 1) * blk)
            s_rows = slice(j * blk, j * blk + half)
            b_ref = b[ref_row:ref_row + 1, :]
            qz = (q[t_rows] * jnp.exp(b[t_rows] - b_ref)).astype(BF16)
            kz = (k[s_rows] * jnp.exp(b_ref - b[s_rows])).astype(BF16)
            p = _dot_nt(qz, kz).astype(BF16)
            for ti in range(half // TILE):
                for si in range(half // TILE):
                    a_tiles[(j * blk + half) // TILE + ti][(j * blk) // TILE + si] = (
                        p[ti * TILE:(ti + 1) * TILE, si * TILE:(si + 1) * TILE])
        half //= 2

    row = lax.broadcasted_iota(jnp.int32, (TILE, TILE), 0)
    col = lax.broadcasted_iota(jnp.int32, (TILE, TILE), 1)
    e_sub = b - _row_blocks(b, SUB, lambda i: None if i == 0 else i * SUB - 1)
    q_parts = [(q * jnp.exp(e_sub)).astype(BF16)]
    k_parts = [(k * jnp.exp(-e_sub)).astype(BF16)]
    masks = [(row // SUB == col // SUB) & (col <= row)]
    half = SUB
    while half < TILE:
        blk = 2 * half
        z = jnp.exp(-jnp.abs(b - _row_blocks(b, blk, lambda j, blk=blk, half=half: j * blk + half - 1)))
        q_parts.append((q * z).astype(BF16))
        k_parts.append((k * z).astype(BF16))
        masks.append((row // blk == col // blk) & (row % blk >= half) & (col % blk < half))
        half = blk
    for i in range(n_tiles):
        rows = slice(i * TILE, (i + 1) * TILE)
        scores = jnp.zeros((TILE, TILE), F32)
        for qp, kp, msk in zip(q_parts, k_parts, masks):
            scores = jnp.where(msk, _dot_nt(qp[rows], kp[rows]), scores)
        a_tiles[i][i] = scores.astype(BF16)

    st = st_ref[...]
    o_state = _dot_nt((q * jnp.exp(b)).astype(BF16), st.astype(BF16))
    outs = []
    for i in range(n_tiles):
        a_row = jnp.concatenate(a_tiles[i][0:i + 1], axis=1)
        outs.append(_dot(a_row, v_bf[0:(i + 1) * TILE]))
    o = jnp.concatenate(outs, axis=0) + o_state

    k_hat = (k * jnp.exp(b_last - b)).astype(BF16)
    st_ref[...] = st * jnp.exp(b_last) + _dot(v.T.astype(BF16), k_hat)

    ms = jnp.mean(o * o, axis=-1, keepdims=True)
    o_ref[...] = (o * lax.rsqrt(ms + NORM_EPS) * gate).astype(o_ref.dtype)


def _hgrn_mixer(hn, w_in, lb_logits, g_norm, *, w_layer, layer_idx, tm=512):
    s, d = hn.shape
    n_heads = w_in.shape[2] // (4 * HEAD)
    n_lb = lb_logits.shape[0]
    kern = functools.partial(_hgrn_kernel, layer_idx, tm // CHUNK)

    def w_spec(part):
        return pl.BlockSpec((None, d, HEAD), lambda h, m: (w_layer, 0, part * n_heads + h))

    return pl.pallas_call(
        kern,
        out_shape=jax.ShapeDtypeStruct((s, n_heads * HEAD), BF16),
        grid=(n_heads, s // tm),
        in_specs=[pl.BlockSpec((tm, d), lambda h, m: (m, 0)),
                  w_spec(0), w_spec(1), w_spec(2), w_spec(3),
                  pl.BlockSpec((n_lb, HEAD), lambda h, m: (0, h)),
                  pl.BlockSpec((1, HEAD), lambda h, m: (0, 0))],
        out_specs=pl.BlockSpec((tm, HEAD), lambda h, m: (m, h)),
        scratch_shapes=[pltpu.VMEM((d, 4 * HEAD), BF16), pltpu.VMEM((HEAD, HEAD), F32)],
        compiler_params=_params(2),
        name="hgrn_mixer",
    )(hn, w_in, w_in, w_in, w_in, lb_logits, g_norm.reshape(1, HEAD))


def _mm_res_norm_kernel(n_k, emit_h, a_ref, w_ref, r_ref, g_ref, *out_refs):
    if emit_h:
        h_ref, hn_ref = out_refs
        acc_ref = h_ref
    else:
        hn_ref, acc_ref = out_refs
    kk = pl.program_id(1)

    @pl.when(kk == 0)
    def _():
        acc_ref[...] = r_ref[...]

    acc_ref[...] += _dot(a_ref[...], w_ref[...])

    @pl.when(kk == n_k - 1)
    def _():
        hn_ref[...] = _rms_scale(acc_ref[...], g_ref[...]).astype(hn_ref.dtype)


def _mm_res_norm(a, w, resid, gain, *, w_layer, emit_h, tm=512, tk=1024):
    s, kdim = a.shape
    d = w.shape[2]
    tk = min(tk, kdim)
    n_k = kdim // tk
    kern = functools.partial(_mm_res_norm_kernel, n_k, emit_h)
    row_spec = pl.BlockSpec((tm, d), lambda m, k: (m, 0))
    if emit_h:
        out_shape = (jax.ShapeDtypeStruct((s, d), F32), jax.ShapeDtypeStruct((s, d), BF16))
        out_specs = (row_spec, row_spec)
        scratch = []
    else:
        out_shape = jax.ShapeDtypeStruct((s, d), F32)
        out_specs = row_spec
        scratch = [pltpu.VMEM((tm, d), F32)]
    return pl.pallas_call(
        kern,
        out_shape=out_shape,
        grid=(s // tm, n_k),
        in_specs=[pl.BlockSpec((tm, tk), lambda m, k: (m, k)),
                  pl.BlockSpec((None, tk, d), lambda m, k: (w_layer, k, 0)),
                  row_spec,
                  pl.BlockSpec((1, d), lambda m, k: (0, 0))],
        out_specs=out_specs,
        scratch_shapes=scratch,
        compiler_params=_params(2),
        name="matmul_residual_rmsnorm",
    )(a, w, resid, gain.reshape(1, d))


def _mlp_up_kernel(x_ref, w_ref, o_ref, w_bf):
    @pl.when(pl.program_id(1) == 0)
    def _():
        w_bf[...] = w_ref[...].astype(BF16)

    a = jnp.maximum(_dot(x_ref[...], w_bf[...]), 0.0)
    o_ref[...] = (a * a).astype(o_ref.dtype)


def _mlp_up(x, w, *, w_layer, tm=1024, tn=1024):
    s, d = x.shape
    f = w.shape[2]
    return pl.pallas_call(
        _mlp_up_kernel,
        out_shape=jax.ShapeDtypeStruct((s, f), BF16),
        grid=(f // tn, s // tm),
        in_specs=[pl.BlockSpec((tm, d), lambda n, m: (m, 0)),
                  pl.BlockSpec((None, d, tn), lambda n, m: (w_layer, 0, n))],
        out_specs=pl.BlockSpec((tm, tn), lambda n, m: (m, n)),
        scratch_shapes=[pltpu.VMEM((d, tn), BF16)],
        compiler_params=_params(2),
        name="mlp_up_relu2",
    )(x, w)


def _gmlp_kernel(n_blocks, hn_ref, w_ref, lng_ref, lnb_ref, wsp_ref, bsp_ref, o_ref):
    half = lng_ref.shape[1]
    z = _dot(hn_ref[...], w_ref[...])
    z = 0.5 * z * (1.0 + lax.erf(z * (1.0 / math.sqrt(2.0))))
    u = z[:, 0:half]
    v = z[:, half:2 * half]
    mu = jnp.mean(v, axis=-1, keepdims=True)
    vc = v - mu
    var = jnp.mean(vc * vc, axis=-1, keepdims=True)
    vn = (vc * lax.rsqrt(var + NORM_EPS) * lng_ref[...] + lnb_ref[...]).astype(BF16)

    row = lax.broadcasted_iota(jnp.int32, (GMLP_BLOCK, GMLP_BLOCK), 0)
    col = lax.broadcasted_iota(jnp.int32, (GMLP_BLOCK, GMLP_BLOCK), 1)
    causal = (col // CHUNK) <= (row // CHUNK)
    gd = half // N_GROUPS
    for g in range(N_GROUPS):
        wg = jnp.where(causal, wsp_ref[g], 0.0).astype(BF16)
        bias = bsp_ref[:, g:g + 1]
        cs = slice(g * gd, (g + 1) * gd)
        for r in range(n_blocks):
            rs = slice(r * GMLP_BLOCK, (r + 1) * GMLP_BLOCK)
            mixed = _dot(wg, vn[rs, cs]) + bias
            o_ref[rs, cs] = (u[rs, cs] * mixed).astype(o_ref.dtype)


def _gmlp_mixer(hn, w_in, ln_gain, ln_bias, w_spatial, b_spatial_t, *, w_layer, tm=256):
    s, d = hn.shape
    two_half = w_in.shape[2]
    half = two_half // 2
    kern = functools.partial(_gmlp_kernel, tm // GMLP_BLOCK)
    const2 = lambda m: (0, 0)
    return pl.pallas_call(
        kern,
        out_shape=jax.ShapeDtypeStruct((s, half), BF16),
        grid=(s // tm,),
        in_specs=[pl.BlockSpec((tm, d), lambda m: (m, 0)),
                  pl.BlockSpec((None, d, two_half), lambda m: (w_layer, 0, 0), pipeline_mode=pl.Buffered(1)),
                  pl.BlockSpec((1, half), const2),
                  pl.BlockSpec((1, half), const2),
                  pl.BlockSpec((None, N_GROUPS, GMLP_BLOCK, GMLP_BLOCK), lambda m: (w_layer, 0, 0, 0)),
                  pl.BlockSpec((GMLP_BLOCK, N_GROUPS), const2)],
        out_specs=pl.BlockSpec((tm, half), lambda m: (m, 0)),
        compiler_params=_params(1),
        name="gmlp_mixer",
    )(hn, w_in, ln_gain.reshape(1, half), ln_bias.reshape(1, half), w_spatial, b_spatial_t)


def kernel(x, norm_mix, norm_mlp, final_norm, hgrn_w_in, hgrn_w_out, hgrn_g_norm, hgrn_lb_logits,
           gmlp_w_in, gmlp_w_out, gmlp_ln_gain, gmlp_ln_bias, gmlp_w_spatial, gmlp_b_spatial,
           mlp_w1, mlp_w2):
    b, s, d = x.shape
    depth = norm_mix.shape[0]
    h = x.reshape(b * s, d)
    assert b == 1, "sequence mixing assumes a single sequence"

    hgrn_w_out_bf = hgrn_w_out.astype(BF16)
    gmlp_w_in_bf = gmlp_w_in.astype(BF16)
    gmlp_w_out_bf = gmlp_w_out.astype(BF16)
    mlp_w2_bf = mlp_w2.astype(BF16)

    hn = _rmsnorm(h, norm_mix[0])
    out = None
    for i in range(depth):
        j = i // 2
        if i % 2 == 0:
            mix = _hgrn_mixer(hn, hgrn_w_in, hgrn_lb_logits, hgrn_g_norm[j], w_layer=j, layer_idx=i)
            w_out = hgrn_w_out_bf
        else:
            mix = _gmlp_mixer(hn, gmlp_w_in_bf, gmlp_ln_gain[j], gmlp_ln_bias[j],
                              gmlp_w_spatial, gmlp_b_spatial[j].T, w_layer=j)
            w_out = gmlp_w_out_bf
        h, hn = _mm_res_norm(mix, w_out, h, norm_mlp[i], w_layer=j, emit_h=True)
        a = _mlp_up(hn, mlp_w1, w_layer=i)
        if i + 1 < depth:
            h, hn = _mm_res_norm(a, mlp_w2_bf, h, norm_mix[i + 1], w_layer=i, emit_h=True)
        else:
            out = _mm_res_norm(a, mlp_w2_bf, h, final_norm, w_layer=i, emit_h=False)
    return out.reshape(b, s, d)
```

```python
import functools
import math

import jax
import jax.numpy as jnp
from jax import lax
from jax.experimental import pallas as pl
from jax.experimental.pallas import tpu as pltpu

F32 = jnp.float32
BF16 = jnp.bfloat16

NORM_EPS = 1e-6
CHUNK = 64
SUB = 16
TILE = 128
HEADS_PER_STEP = 2
PROJ_PIECES = ((1, 256), (3, 128), (5, 128))
HEAD = 128
GMLP_BLOCK = 128
N_GROUPS = 16
BF16_ROWS = 16
EPILOGUE_ROWS = 256

VMEM_LIMIT_BYTES = 56 * 1024 * 1024
RESIDENT_WEIGHT_BYTES = 8 * 1024 * 1024


def _dot(a, b):
    return jnp.dot(a, b, preferred_element_type=F32)


def _dot_nt(a, b):
    return lax.dot_general(a, b, (((1,), (1,)), ((), ())), preferred_element_type=F32)


def _sigmoid(x):
    return 1.0 / (1.0 + jnp.exp(-x))


def _rms_scale(h, gain):
    ms = jnp.mean(h * h, axis=-1, keepdims=True)
    return h * lax.rsqrt(ms + NORM_EPS) * gain


def _params(n_axes):
    return pltpu.CompilerParams(
        dimension_semantics=("arbitrary",) * n_axes, vmem_limit_bytes=VMEM_LIMIT_BYTES)


def _rmsnorm_kernel(x_ref, g_ref, o_ref):
    o_ref[...] = _rms_scale(x_ref[...], g_ref[...]).astype(o_ref.dtype)


def _rmsnorm(x, gain, *, tm=1024):
    s, d = x.shape
    return pl.pallas_call(
        _rmsnorm_kernel,
        out_shape=jax.ShapeDtypeStruct((s, d), BF16),
        grid=(s // tm,),
        in_specs=[pl.BlockSpec((tm, d), lambda m: (m, 0)),
                  pl.BlockSpec((1, d), lambda m: (0, 0))],
        out_specs=pl.BlockSpec((tm, d), lambda m: (m, 0)),
        compiler_params=_params(1),
        name="rmsnorm",
    )(x, gain.reshape(1, d))


def _split3_bf16(x):
    hi = x.astype(BF16)
    r1 = x - hi.astype(F32)
    mid = r1.astype(BF16)
    lo = (r1 - mid.astype(F32)).astype(BF16)
    return hi, mid, lo


def _row_blocks(src, blk, row_of_block):
    parts = []
    for j in range(src.shape[0] // blk):
        r = row_of_block(j)
        parts.append(jnp.broadcast_to(src[r:r + 1, :], (blk, src.shape[1])))
    return jnp.concatenate(parts, axis=0)


def _hgrn_project(layer_idx, rows, hn_ref, w_bf, lbl_ref, gn_ref, act_ref):
    lg = lbl_ref[...]
    lmax = jnp.max(lg, axis=0, keepdims=True)
    le = jnp.exp(lg - lmax)
    lb = jnp.sum(le[0:layer_idx + 1], axis=0, keepdims=True) / jnp.sum(le, axis=0, keepdims=True)

    width = act_ref.shape[2]
    proj = _dot(hn_ref[rows, :], w_bf[...])
    pq = proj[:, 0:width]
    pf = proj[:, width:2 * width]
    pg = proj[:, 3 * width:4 * width]
    forget = lb + (1.0 - lb) * _sigmoid(pf)
    act_ref[0, rows, :] = pq * _sigmoid(pq)
    act_ref[1, rows, :] = 1.0 - forget
    act_ref[2, rows, :] = proj[:, 2 * width:3 * width]
    act_ref[3, rows, :] = jnp.log(forget)
    act_ref[4, rows, :] = pg * _sigmoid(pg) * gn_ref[...]


def _hgrn_kernel(layer_idx, n_chunks, n_m, n_side, hn_ref, wq_ref, wf_ref, wi_ref, wg_ref, lbl_ref, gn_ref,
                 *refs):
    side_in, o_ref, side_out = refs[:n_side], refs[n_side], refs[n_side + 1:2 * n_side + 1]
    w_bf, st_ref, act_a, act_b = refs[2 * n_side + 1:]
    t = pl.program_id(0)
    n_blocks = pl.num_programs(0) - 1
    m_proj = lax.rem(jnp.minimum(t, n_blocks - 1), n_m)
    m_rec = lax.rem(jnp.maximum(t - 1, 0), n_m)

    @pl.when(t == 0)
    def _():
        act_b[...] = jnp.zeros_like(act_b)

    @pl.when(m_proj == 0)
    def _():
        width = act_a.shape[2]
        for p, r in enumerate((wq_ref, wf_ref, wi_ref, wg_ref)):
            w_bf[:, p * width:(p + 1) * width] = r[...].astype(BF16)

    @pl.when(m_rec == 0)
    def _():
        st_ref[...] = jnp.zeros_like(st_ref)

    for parity, (act_new, act_old) in enumerate(((act_a, act_b), (act_b, act_a))):
        @pl.when(lax.rem(t, 2) == parity)
        def _():
            heads = [_hgrn_recurrence(n_chunks, hl, act_old, o_ref, st_ref) for hl in range(HEADS_PER_STEP)]
            stage = 0
            row0 = 0
            for after, piece in PROJ_PIECES:
                while stage < after:
                    for stages in heads:
                        next(stages, None)
                    if stage < n_side:
                        side_out[stage][...] = side_in[stage][...].astype(BF16)
                    stage += 1
                _hgrn_project(layer_idx, slice(row0, row0 + piece), hn_ref, w_bf, lbl_ref, gn_ref, act_new)
                row0 += piece
            assert row0 == n_chunks * CHUNK
            for stages in heads:
                for _ in stages:
                    pass
            for i in range(stage, n_side):
                side_out[i][...] = side_in[i][...].astype(BF16)


def _hgrn_recurrence(n_chunks, hl, act_ref, o_ref, st_ref):
    lanes = slice(hl * HEAD, (hl + 1) * HEAD)
    q, k, v, logf, gate = (act_ref[i, :, lanes] for i in range(5))
    tm = n_chunks * CHUNK
    n_tiles = tm // TILE

    def chunk(x, c):
        return x[c * CHUNK:(c + 1) * CHUNK]

    def chunks_to_lanes(x):
        return jnp.concatenate([chunk(x, c) for c in range(n_chunks)], axis=1)

    row_c = lax.broadcasted_iota(jnp.int32, (CHUNK, CHUNK), 0)
    col_c = lax.broadcasted_iota(jnp.int32, (CHUNK, CHUNK), 1)
    tril = (col_c <= row_c).astype(BF16)
    hi, mid, lo = _split3_bf16(logf)
    cs = _dot(tril, jnp.concatenate([chunks_to_lanes(hi), chunks_to_lanes(mid), chunks_to_lanes(lo)], axis=1))
    wl = n_chunks * HEAD
    bl = cs[:, 0:wl] + cs[:, wl:2 * wl] + cs[:, 2 * wl:3 * wl]
    b_parts = []
    offset = None
    for c in range(n_chunks):
        bc = bl[:, c * HEAD:(c + 1) * HEAD]
        if offset is not None:
            bc = bc + offset
        b_parts.append(bc)
        offset = bc[CHUNK - 1:CHUNK, :]
    b = jnp.concatenate(b_parts, axis=0)
    b_last = offset
    yield

    v_bf = v.astype(BF16)
    a_tiles = [[None] * n_tiles for _ in range(n_tiles)]

    half = tm // 2
    while half >= TILE:
        blk = 2 * half
        for j in range(tm // blk):
            ref_row = j * blk + half - 1
            t_rows = slice(j * blk + half, (j + 1) * blk)
            s_rows = slice(j * blk, j * blk + half)
            b_ref = b[ref_row:ref_row + 1, :]
            qz = (q[t_rows] * jnp.exp(b[t_rows] - b_ref)).astype(BF16)
            kz = (k[s_rows] * jnp.exp(b_ref - b[s_rows])).astype(BF16)
            p = _dot_nt(qz, kz).astype(BF16)
            for ti in range(half // TILE):
                for si in range(half // TILE):
                    a_tiles[(j * blk + half) // TILE + ti][(j * blk) // TILE + si] = (
                        p[ti * TILE:(ti + 1) * TILE, si * TILE:(si + 1) * TILE])
        half //= 2
    yield

    row = lax.broadcasted_iota(jnp.int32, (TILE, TILE), 0)
    col = lax.broadcasted_iota(jnp.int32, (TILE, TILE), 1)
    e_sub = b - _row_blocks(b, SUB, lambda i: i * SUB + SUB // 2 - 1)
    q_parts = [(q * jnp.exp(e_sub)).astype(BF16)]
    k_parts = [(k * jnp.exp(-e_sub)).astype(BF16)]
    masks = [(row // SUB == col // SUB) & (col <= row)]
    half = SUB
    while half < TILE:
        blk = 2 * half
        z = jnp.exp(-jnp.abs(b - _row_blocks(b, blk, lambda j, blk=blk, half=half: j * blk + half - 1)))
        q_parts.append((q * z).astype(BF16))
        k_parts.append((k * z).astype(BF16))
        masks.append((row // blk == col // blk) & (row % blk >= half) & (col % blk < half))
        half = blk
    yield
    for i in range(n_tiles):
        rows = slice(i * TILE, (i + 1) * TILE)
        scores = jnp.zeros((TILE, TILE), F32)
        for qp, kp, msk in zip(q_parts, k_parts, masks):
            scores = jnp.where(msk, _dot_nt(qp[rows], kp[rows]), scores)
        a_tiles[i][i] = scores.astype(BF16)
    yield

    st = st_ref[hl]
    o_state = _dot_nt((q * jnp.exp(b)).astype(BF16), st.astype(BF16))
    outs = []
    for i in range(n_tiles):
        a_row = jnp.concatenate(a_tiles[i][0:i + 1], axis=1)
        outs.append(_dot(a_row, v_bf[0:(i + 1) * TILE]))
    o = jnp.concatenate(outs, axis=0) + o_state

    k_hat = (k * jnp.exp(b_last - b)).astype(BF16)
    st_ref[hl] = st * jnp.exp(b_last) + _dot(v.T.astype(BF16), k_hat)

    ms = jnp.mean(o * o, axis=-1, keepdims=True)
    o_ref[:, lanes] = (o * lax.rsqrt(ms + NORM_EPS) * gate).astype(o_ref.dtype)


def _hgrn_mixer(hn, w_in, lb_logits, g_norm, *, w_layer, layer_idx, side_weights=(), tm=512):
    s, d = hn.shape
    n_heads = w_in.shape[2] // (4 * HEAD)
    n_lb = lb_logits.shape[0]
    n_m = s // tm
    width = HEADS_PER_STEP * HEAD
    n_groups = n_heads // HEADS_PER_STEP
    n_blocks = n_groups * n_m
    kern = functools.partial(_hgrn_kernel, layer_idx, tm // CHUNK, n_m, len(side_weights))

    def proj_block(t):
        tp = jnp.minimum(t, n_blocks - 1)
        return tp // n_m, lax.rem(tp, n_m)

    def rec_block(t):
        tr = jnp.maximum(t - 1, 0)
        return tr // n_m, lax.rem(tr, n_m)

    def w_spec(part):
        return pl.BlockSpec((None, d, width), lambda t: (w_layer, 0, part * n_groups + proj_block(t)[0]))

    side_2d, side_in_specs, side_out_specs, side_shapes = [], [], [], []
    for sw in side_weights:
        cols = sw.shape[-1]
        rows = sw.size // cols
        blk = rows // n_blocks
        assert blk * n_blocks == rows and blk % BF16_ROWS == 0, (sw.shape, n_blocks)
        side_2d.append(sw.reshape(rows, cols))
        for specs in (side_in_specs, side_out_specs):
            specs.append(pl.BlockSpec((blk, cols), lambda t: (jnp.minimum(t, n_blocks - 1), 0)))
        side_shapes.append(jax.ShapeDtypeStruct((rows, cols), BF16))

    act = pltpu.VMEM((5, tm, width), F32)
    outs = pl.pallas_call(
        kern,
        out_shape=[jax.ShapeDtypeStruct((s, n_heads * HEAD), BF16)] + side_shapes,
        grid=(n_blocks + 1,),
        in_specs=[pl.BlockSpec((tm, d), lambda t: (proj_block(t)[1], 0)),
                  w_spec(0), w_spec(1), w_spec(2), w_spec(3),
                  pl.BlockSpec((n_lb, width), lambda t: (0, proj_block(t)[0])),
                  pl.BlockSpec((1, width), lambda t: (0, 0))] + side_in_specs,
        out_specs=[pl.BlockSpec((tm, width), lambda t: (rec_block(t)[1], rec_block(t)[0]))] + side_out_specs,
        scratch_shapes=[pltpu.VMEM((d, 4 * width), BF16), pltpu.VMEM((HEADS_PER_STEP, HEAD, HEAD), F32),
                        act, act],
        compiler_params=_params(1),
        name="hgrn_mixer",
    )(hn, w_in, w_in, w_in, w_in, lb_logits, jnp.tile(g_norm.reshape(1, HEAD), (1, HEADS_PER_STEP)), *side_2d)
    return outs[0], [o.reshape(sw.shape) for o, sw in zip(outs[1:], side_weights)]


def _mm_res_norm_kernel(n_k, emit_h, n_pieces, a_ref, w_ref, r_ref, g_ref, *out_refs):
    if emit_h:
        h_ref, hn_ref = out_refs
        acc_ref = h_ref
    else:
        hn_ref, acc_ref = out_refs
    kk = pl.program_id(1)
    piece = a_ref.shape[0] // n_pieces
    res_rows = r_ref.shape[0]

    def step(first, last):
        if not last:
            part = _dot(a_ref[...], w_ref[...])
            acc_ref[...] = part if first else acc_ref[...] + part
            rows = pl.ds(pl.multiple_of(kk * res_rows, res_rows), res_rows)
            acc_ref[rows, :] = acc_ref[rows, :] + r_ref[...]
            return
        for p in range(n_pieces):
            rows = slice(p * piece, (p + 1) * piece)
            part = _dot(a_ref[rows, :], w_ref[...])
            h = part if first else acc_ref[rows, :] + part
            lo = max(rows.start, (n_k - 1) * res_rows)
            if lo < rows.stop:
                res = r_ref[lo - (n_k - 1) * res_rows:rows.stop - (n_k - 1) * res_rows, :]
                if lo > rows.start:
                    res = jnp.concatenate([jnp.zeros((lo - rows.start, res.shape[1]), F32), res], axis=0)
                h = h + res
            if emit_h:
                h_ref[rows, :] = h
            hn_ref[rows, :] = _rms_scale(h, g_ref[...]).astype(hn_ref.dtype)

    if n_k == 1:
        step(True, True)
    else:
        pl.when(kk == 0)(lambda: step(True, False))
        pl.when((kk > 0) & (kk < n_k - 1))(lambda: step(False, False))
        pl.when(kk == n_k - 1)(lambda: step(False, True))


def _mm_res_norm_tiles(kdim, d):
    if kdim * d * jnp.dtype(BF16).itemsize <= RESIDENT_WEIGHT_BYTES:
        return 512, kdim
    return 1024, 2048


def _mm_res_norm(a, w, resid, gain, *, w_layer, emit_h):
    s, kdim = a.shape
    d = w.shape[2]
    tm, tk = _mm_res_norm_tiles(kdim, d)
    n_k = kdim // tk
    kern = functools.partial(_mm_res_norm_kernel, n_k, emit_h, tm // EPILOGUE_ROWS)
    row_spec = pl.BlockSpec((tm, d), lambda m, k: (m, 0))
    if emit_h:
        out_shape = (jax.ShapeDtypeStruct((s, d), F32), jax.ShapeDtypeStruct((s, d), BF16))
        out_specs = (row_spec, row_spec)
        scratch = []
    else:
        out_shape = jax.ShapeDtypeStruct((s, d), F32)
        out_specs = row_spec
        scratch = [pltpu.VMEM((tm, d), F32)]
    return pl.pallas_call(
        kern,
        out_shape=out_shape,
        grid=(s // tm, n_k),
        in_specs=[pl.BlockSpec((tm, tk), lambda m, k: (m, k)),
                  pl.BlockSpec((None, tk, d), lambda m, k: (w_layer, k, 0)),
                  pl.BlockSpec((tm // n_k, d), lambda m, k: (m * n_k + k, 0)),
                  pl.BlockSpec((1, d), lambda m, k: (0, 0))],
        out_specs=out_specs,
        scratch_shapes=scratch,
        compiler_params=_params(2),
        name="matmul_residual_rmsnorm",
    )(a, w, resid, gain.reshape(1, d))


def _mlp_up_kernel(x_ref, w_ref, o_ref):
    a = jnp.maximum(_dot(x_ref[...], w_ref[...]), 0.0)
    o_ref[...] = (a * a).astype(o_ref.dtype)


def _mlp_up(x, w, *, w_layer, tm=1024, tn=2048):
    s, d = x.shape
    f = w.shape[2]
    return pl.pallas_call(
        _mlp_up_kernel,
        out_shape=jax.ShapeDtypeStruct((s, f), BF16),
        grid=(f // tn, s // tm),
        in_specs=[pl.BlockSpec((tm, d), lambda n, m: (m, 0)),
                  pl.BlockSpec((None, d, tn), lambda n, m: (w_layer, 0, n))],
        out_specs=pl.BlockSpec((tm, tn), lambda n, m: (m, n)),
        compiler_params=_params(2),
        name="mlp_up_relu2",
    )(x, w)


def _gmlp_kernel(n_pieces, hn_ref, w_ref, lng_ref, lnb_ref, wsp_ref, bsp_ref, o_ref):
    half = lng_ref.shape[1]
    piece = hn_ref.shape[0] // n_pieces
    row = lax.broadcasted_iota(jnp.int32, (GMLP_BLOCK, GMLP_BLOCK), 0)
    col = lax.broadcasted_iota(jnp.int32, (GMLP_BLOCK, GMLP_BLOCK), 1)
    causal = (col // CHUNK) <= (row // CHUNK)
    gd = half // N_GROUPS
    for p in range(n_pieces):
        z = _dot(hn_ref[p * piece:(p + 1) * piece, :], w_ref[...])
        z = 0.5 * z * (1.0 + lax.erf(z * (1.0 / math.sqrt(2.0))))
        u = z[:, 0:half]
        v = z[:, half:2 * half]
        mu = jnp.mean(v, axis=-1, keepdims=True)
        vc = v - mu
        var = jnp.mean(vc * vc, axis=-1, keepdims=True)
        vn = (vc * lax.rsqrt(var + NORM_EPS) * lng_ref[...] + lnb_ref[...]).astype(BF16)
        for g in range(N_GROUPS):
            wg = jnp.where(causal, wsp_ref[g], 0.0).astype(BF16)
            bias = bsp_ref[:, g:g + 1]
            cs = slice(g * gd, (g + 1) * gd)
            for r in range(piece // GMLP_BLOCK):
                rs = slice(r * GMLP_BLOCK, (r + 1) * GMLP_BLOCK)
                mixed = _dot(wg, vn[rs, cs]) + bias
                o_ref[p * piece + r * GMLP_BLOCK:p * piece + (r + 1) * GMLP_BLOCK, cs] = (
                    u[rs, cs] * mixed).astype(o_ref.dtype)


def _gmlp_mixer(hn, w_in, ln_gain, ln_bias, w_spatial, b_spatial_t, *, w_layer, tm=512, n_pieces=2):
    s, d = hn.shape
    two_half = w_in.shape[2]
    half = two_half // 2
    kern = functools.partial(_gmlp_kernel, n_pieces)
    const2 = lambda m: (0, 0)
    return pl.pallas_call(
        kern,
        out_shape=jax.ShapeDtypeStruct((s, half), BF16),
        grid=(s // tm,),
        in_specs=[pl.BlockSpec((tm, d), lambda m: (m, 0)),
                  pl.BlockSpec((None, d, two_half), lambda m: (w_layer, 0, 0), pipeline_mode=pl.Buffered(1)),
                  pl.BlockSpec((1, half), const2),
                  pl.BlockSpec((1, half), const2),
                  pl.BlockSpec((None, N_GROUPS, GMLP_BLOCK, GMLP_BLOCK), lambda m: (w_layer, 0, 0, 0)),
                  pl.BlockSpec((GMLP_BLOCK, N_GROUPS), const2)],
        out_specs=pl.BlockSpec((tm, half), lambda m: (m, 0)),
        compiler_params=_params(1),
        name="gmlp_mixer",
    )(hn, w_in, ln_gain.reshape(1, half), ln_bias.reshape(1, half), w_spatial, b_spatial_t)


def kernel(x, norm_mix, norm_mlp, final_norm, hgrn_w_in, hgrn_w_out, hgrn_g_norm, hgrn_lb_logits,
           gmlp_w_in, gmlp_w_out, gmlp_ln_gain, gmlp_ln_bias, gmlp_w_spatial, gmlp_b_spatial,
           mlp_w1, mlp_w2):
    b, s, d = x.shape
    depth = norm_mix.shape[0]
    h = x.reshape(b * s, d)
    assert b == 1, "sequence mixing assumes a single sequence"

    hn = _rmsnorm(h, norm_mix[0])
    side = (hgrn_w_out, gmlp_w_in, gmlp_w_out, mlp_w1, mlp_w2)
    mix, (hgrn_w_out_bf, gmlp_w_in_bf, gmlp_w_out_bf, mlp_w1_bf, mlp_w2_bf) = _hgrn_mixer(
        hn, hgrn_w_in, hgrn_lb_logits, hgrn_g_norm[0], w_layer=0, layer_idx=0, side_weights=side)
    out = None
    for i in range(depth):
        j = i // 2
        if i % 2 == 0:
            if i > 0:
                mix, _ = _hgrn_mixer(hn, hgrn_w_in, hgrn_lb_logits, hgrn_g_norm[j], w_layer=j, layer_idx=i)
            w_out = hgrn_w_out_bf
        else:
            mix = _gmlp_mixer(hn, gmlp_w_in_bf, gmlp_ln_gain[j], gmlp_ln_bias[j],
                              gmlp_w_spatial, gmlp_b_spatial[j].T, w_layer=j)
            w_out = gmlp_w_out_bf
        h, hn = _mm_res_norm(mix, w_out, h, norm_mlp[i], w_layer=j, emit_h=True)
        a = _mlp_up(hn, mlp_w1_bf, w_layer=i)
        if i + 1 < depth:
            h, hn = _mm_res_norm(a, mlp_w2_bf, h, norm_mix[i + 1], w_layer=i, emit_h=True)
        else:
            out = _mm_res_norm(a, mlp_w2_bf, h, final_norm, w_layer=i, emit_h=False)
    return out.reshape(b, s, d)
```

```python
import functools
import math

import jax
import jax.numpy as jnp
from jax import lax
from jax.experimental import pallas as pl
from jax.experimental.pallas import tpu as pltpu

F32 = jnp.float32
BF16 = jnp.bfloat16

NORM_EPS = 1e-6
CHUNK = 64
SUB = 16
TILE = 128
HEADS_PER_STEP = 2
PROJ_PIECES = ((1, 256), (2, 128), (5, 128))
HEAD = 128
GMLP_BLOCK = 128
N_GROUPS = 16
BF16_ROWS = 16
EPILOGUE_ROWS = 256

VMEM_LIMIT_BYTES = 56 * 1024 * 1024
RESIDENT_WEIGHT_BYTES = 8 * 1024 * 1024


def _dot(a, b):
    return jnp.dot(a, b, preferred_element_type=F32)


def _dot_nt(a, b):
    return lax.dot_general(a, b, (((1,), (1,)), ((), ())), preferred_element_type=F32)


def _sigmoid(x):
    return 1.0 / (1.0 + jnp.exp(-x))


def _rms_scale(h, gain):
    ms = jnp.mean(h * h, axis=-1, keepdims=True)
    return h * lax.rsqrt(ms + NORM_EPS) * gain


def _params(n_axes):
    return pltpu.CompilerParams(
        dimension_semantics=("arbitrary",) * n_axes, vmem_limit_bytes=VMEM_LIMIT_BYTES)


def _rmsnorm_kernel(x_ref, g_ref, o_ref):
    o_ref[...] = _rms_scale(x_ref[...], g_ref[...]).astype(o_ref.dtype)


def _rmsnorm(x, gain, *, tm=2048):
    s, d = x.shape
    return pl.pallas_call(
        _rmsnorm_kernel,
        out_shape=jax.ShapeDtypeStruct((s, d), BF16),
        grid=(s // tm,),
        in_specs=[pl.BlockSpec((tm, d), lambda m: (m, 0)),
                  pl.BlockSpec((1, d), lambda m: (0, 0))],
        out_specs=pl.BlockSpec((tm, d), lambda m: (m, 0)),
        compiler_params=_params(1),
        name="rmsnorm",
    )(x, gain.reshape(1, d))


def _split3_bf16(x):
    hi = x.astype(BF16)
    r1 = x - hi.astype(F32)
    mid = r1.astype(BF16)
    lo = (r1 - mid.astype(F32)).astype(BF16)
    return hi, mid, lo


def _row_blocks(src, blk, row_of_block):
    parts = []
    for j in range(src.shape[0] // blk):
        r = row_of_block(j)
        parts.append(jnp.broadcast_to(src[r:r + 1, :], (blk, src.shape[1])))
    return jnp.concatenate(parts, axis=0)


def _hgrn_project(layer_idx, rows, hn_ref, w_bf, lbl_ref, gn_ref, act_ref):
    lg = lbl_ref[...]
    lmax = jnp.max(lg, axis=0, keepdims=True)
    le = jnp.exp(lg - lmax)
    lb = jnp.sum(le[0:layer_idx + 1], axis=0, keepdims=True) / jnp.sum(le, axis=0, keepdims=True)

    width = act_ref.shape[2]
    proj = _dot(hn_ref[rows, :], w_bf[...])
    pq = proj[:, 0:width]
    pf = proj[:, width:2 * width]
    pg = proj[:, 3 * width:4 * width]
    forget = lb + (1.0 - lb) * _sigmoid(pf)
    act_ref[0, rows, :] = pq * _sigmoid(pq)
    act_ref[1, rows, :] = 1.0 - forget
    act_ref[2, rows, :] = proj[:, 2 * width:3 * width]
    act_ref[3, rows, :] = jnp.log(forget)
    act_ref[4, rows, :] = pg * _sigmoid(pg) * gn_ref[...]


def _hgrn_kernel(layer_idx, n_chunks, n_m, n_side, hn_ref, wq_ref, wf_ref, wi_ref, wg_ref, lbl_ref, gn_ref,
                 *refs):
    side_in, o_ref, side_out = refs[:n_side], refs[n_side], refs[n_side + 1:2 * n_side + 1]
    w_bf, st_ref, act_a, act_b = refs[2 * n_side + 1:]
    t = pl.program_id(0)
    n_blocks = pl.num_programs(0) - 1
    m_proj = lax.rem(jnp.minimum(t, n_blocks - 1), n_m)
    m_rec = lax.rem(jnp.maximum(t - 1, 0), n_m)

    @pl.when(t == 0)
    def _():
        act_b[...] = jnp.zeros_like(act_b)

    @pl.when(m_proj == 0)
    def _():
        width = act_a.shape[2]
        for p, r in enumerate((wq_ref, wf_ref, wi_ref, wg_ref)):
            w_bf[:, p * width:(p + 1) * width] = r[...].astype(BF16)

    @pl.when(m_rec == 0)
    def _():
        st_ref[...] = jnp.zeros_like(st_ref)

    for parity, (act_new, act_old) in enumerate(((act_a, act_b), (act_b, act_a))):
        @pl.when(lax.rem(t, 2) == parity)
        def _():
            heads = [_hgrn_recurrence(n_chunks, hl, act_old, o_ref, st_ref) for hl in range(HEADS_PER_STEP)]
            stage = 0
            row0 = 0
            for after, piece in PROJ_PIECES:
                while stage < after:
                    for stages in heads:
                        next(stages, None)
                    if stage < n_side:
                        side_out[stage][...] = side_in[stage][...].astype(BF16)
                    stage += 1
                _hgrn_project(layer_idx, slice(row0, row0 + piece), hn_ref, w_bf, lbl_ref, gn_ref, act_new)
                row0 += piece
            assert row0 == n_chunks * CHUNK
            for stages in heads:
                for _ in stages:
                    pass
            for i in range(stage, n_side):
                side_out[i][...] = side_in[i][...].astype(BF16)


def _hgrn_recurrence(n_chunks, hl, act_ref, o_ref, st_ref):
    lanes = slice(hl * HEAD, (hl + 1) * HEAD)
    q, k, v, logf, gate = (act_ref[i, :, lanes] for i in range(5))
    tm = n_chunks * CHUNK
    n_tiles = tm // TILE

    def chunk(x, c):
        return x[c * CHUNK:(c + 1) * CHUNK]

    def chunks_to_lanes(x):
        return jnp.concatenate([chunk(x, c) for c in range(n_chunks)], axis=1)

    row_c = lax.broadcasted_iota(jnp.int32, (CHUNK, CHUNK), 0)
    col_c = lax.broadcasted_iota(jnp.int32, (CHUNK, CHUNK), 1)
    tril = (col_c <= row_c).astype(BF16)
    hi, mid, lo = _split3_bf16(logf)
    cs = _dot(tril, jnp.concatenate([chunks_to_lanes(hi), chunks_to_lanes(mid), chunks_to_lanes(lo)], axis=1))
    wl = n_chunks * HEAD
    bl = cs[:, 0:wl] + cs[:, wl:2 * wl] + cs[:, 2 * wl:3 * wl]
    b_parts = []
    offset = None
    for c in range(n_chunks):
        bc = bl[:, c * HEAD:(c + 1) * HEAD]
        if offset is not None:
            bc = bc + offset
        b_parts.append(bc)
        offset = bc[CHUNK - 1:CHUNK, :]
    b = jnp.concatenate(b_parts, axis=0)
    b_last = offset
    yield

    v_bf = v.astype(BF16)
    a_tiles = [[None] * n_tiles for _ in range(n_tiles)]

    half = tm // 2
    while half >= TILE:
        blk = 2 * half
        for j in range(tm // blk):
            ref_row = j * blk + half - 1
            t_rows = slice(j * blk + half, (j + 1) * blk)
            s_rows = slice(j * blk, j * blk + half)
            b_ref = b[ref_row:ref_row + 1, :]
            qz = (q[t_rows] * jnp.exp(b[t_rows] - b_ref)).astype(BF16)
            kz = (k[s_rows] * jnp.exp(b_ref - b[s_rows])).astype(BF16)
            p = _dot_nt(qz, kz).astype(BF16)
            for ti in range(half // TILE):
                for si in range(half // TILE):
                    a_tiles[(j * blk + half) // TILE + ti][(j * blk) // TILE + si] = (
                        p[ti * TILE:(ti + 1) * TILE, si * TILE:(si + 1) * TILE])
        half //= 2
    yield

    row = lax.broadcasted_iota(jnp.int32, (TILE, TILE), 0)
    col = lax.broadcasted_iota(jnp.int32, (TILE, TILE), 1)
    e_sub = b - _row_blocks(b, SUB, lambda i: i * SUB + SUB // 2 - 1)
    q_parts = [(q * jnp.exp(e_sub)).astype(BF16)]
    k_parts = [(k * jnp.exp(-e_sub)).astype(BF16)]
    masks = [(row // SUB == col // SUB) & (col <= row)]
    half = SUB
    while half < TILE:
        blk = 2 * half
        z = jnp.exp(-jnp.abs(b - _row_blocks(b, blk, lambda j, blk=blk, half=half: j * blk + half - 1)))
        q_parts.append((q * z).astype(BF16))
        k_parts.append((k * z).astype(BF16))
        masks.append((row // blk == col // blk) & (row % blk >= half) & (col % blk < half))
        half = blk
    yield
    for i in range(n_tiles):
        rows = slice(i * TILE, (i + 1) * TILE)
        scores = jnp.zeros((TILE, TILE), F32)
        for qp, kp, msk in zip(q_parts, k_parts, masks):
            scores = jnp.where(msk, _dot_nt(qp[rows], kp[rows]), scores)
        a_tiles[i][i] = scores.astype(BF16)
    yield

    st = st_ref[hl]
    o_state = _dot_nt((q * jnp.exp(b)).astype(BF16), st.astype(BF16))
    outs = []
    for i in range(n_tiles):
        a_row = jnp.concatenate(a_tiles[i][0:i + 1], axis=1)
        outs.append(_dot(a_row, v_bf[0:(i + 1) * TILE]))
    o = jnp.concatenate(outs, axis=0) + o_state

    k_hat = (k * jnp.exp(b_last - b)).astype(BF16)
    st_ref[hl] = st * jnp.exp(b_last) + _dot(v.T.astype(BF16), k_hat)

    ms = jnp.mean(o * o, axis=-1, keepdims=True)
    o_ref[:, lanes] = (o * lax.rsqrt(ms + NORM_EPS) * gate).astype(o_ref.dtype)


def _hgrn_mixer(hn, w_in, lb_logits, g_norm, *, w_layer, layer_idx, side_weights=(), tm=512):
    s, d = hn.shape
    n_heads = w_in.shape[2] // (4 * HEAD)
    n_lb = lb_logits.shape[0]
    n_m = s // tm
    width = HEADS_PER_STEP * HEAD
    n_groups = n_heads // HEADS_PER_STEP
    n_blocks = n_groups * n_m
    kern = functools.partial(_hgrn_kernel, layer_idx, tm // CHUNK, n_m, len(side_weights))

    def proj_block(t):
        tp = jnp.minimum(t, n_blocks - 1)
        return tp // n_m, lax.rem(tp, n_m)

    def rec_block(t):
        tr = jnp.maximum(t - 1, 0)
        return tr // n_m, lax.rem(tr, n_m)

    def w_spec(part):
        return pl.BlockSpec((None, d, width), lambda t: (w_layer, 0, part * n_groups + proj_block(t)[0]))

    side_2d, side_in_specs, side_out_specs, side_shapes = [], [], [], []
    for sw in side_weights:
        cols = sw.shape[-1]
        rows = sw.size // cols
        blk = rows // n_blocks
        assert blk * n_blocks == rows and blk % BF16_ROWS == 0, (sw.shape, n_blocks)
        side_2d.append(sw.reshape(rows, cols))
        for specs in (side_in_specs, side_out_specs):
            specs.append(pl.BlockSpec((blk, cols), lambda t: (jnp.minimum(t, n_blocks - 1), 0)))
        side_shapes.append(jax.ShapeDtypeStruct((rows, cols), BF16))

    act = pltpu.VMEM((5, tm, width), F32)
    outs = pl.pallas_call(
        kern,
        out_shape=[jax.ShapeDtypeStruct((s, n_heads * HEAD), BF16)] + side_shapes,
        grid=(n_blocks + 1,),
        in_specs=[pl.BlockSpec((tm, d), lambda t: (proj_block(t)[1], 0)),
                  w_spec(0), w_spec(1), w_spec(2), w_spec(3),
                  pl.BlockSpec((n_lb, width), lambda t: (0, proj_block(t)[0])),
                  pl.BlockSpec((1, width), lambda t: (0, 0))] + side_in_specs,
        out_specs=[pl.BlockSpec((tm, width), lambda t: (rec_block(t)[1], rec_block(t)[0]))] + side_out_specs,
        scratch_shapes=[pltpu.VMEM((d, 4 * width), BF16), pltpu.VMEM((HEADS_PER_STEP, HEAD, HEAD), F32),
                        act, act],
        compiler_params=_params(1),
        name="hgrn_mixer",
    )(hn, w_in, w_in, w_in, w_in, lb_logits, jnp.tile(g_norm.reshape(1, HEAD), (1, HEADS_PER_STEP)), *side_2d)
    return outs[0], [o.reshape(sw.shape) for o, sw in zip(outs[1:], side_weights)]


def _mm_res_norm_kernel(n_k, emit_h, n_pieces, a_ref, w_ref, r_ref, g_ref, *out_refs):
    if emit_h:
        h_ref, hn_ref = out_refs
        acc_ref = h_ref
    else:
        hn_ref, acc_ref = out_refs
    kk = pl.program_id(1)
    piece = a_ref.shape[0] // n_pieces
    res_rows = r_ref.shape[0]

    def step(first, last):
        if not last:
            part = _dot(a_ref[...], w_ref[...])
            acc_ref[...] = part if first else acc_ref[...] + part
            rows = pl.ds(pl.multiple_of(kk * res_rows, res_rows), res_rows)
            acc_ref[rows, :] = acc_ref[rows, :] + r_ref[...]
            return
        for p in range(n_pieces):
            rows = slice(p * piece, (p + 1) * piece)
            part = _dot(a_ref[rows, :], w_ref[...])
            h = part if first else acc_ref[rows, :] + part
            lo = max(rows.start, (n_k - 1) * res_rows)
            if lo < rows.stop:
                res = r_ref[lo - (n_k - 1) * res_rows:rows.stop - (n_k - 1) * res_rows, :]
                if lo > rows.start:
                    res = jnp.concatenate([jnp.zeros((lo - rows.start, res.shape[1]), F32), res], axis=0)
                h = h + res
            if emit_h:
                h_ref[rows, :] = h
            hn_ref[rows, :] = _rms_scale(h, g_ref[...]).astype(hn_ref.dtype)

    if n_k == 1:
        step(True, True)
    else:
        pl.when(kk == 0)(lambda: step(True, False))
        pl.when((kk > 0) & (kk < n_k - 1))(lambda: step(False, False))
        pl.when(kk == n_k - 1)(lambda: step(False, True))


def _mm_res_norm_tiles(kdim, d):
    if kdim * d * jnp.dtype(BF16).itemsize <= RESIDENT_WEIGHT_BYTES:
        return 512, kdim
    return 1024, 2048


def _mm_res_norm(a, w, resid, gain, *, w_layer, emit_h):
    s, kdim = a.shape
    d = w.shape[2]
    tm, tk = _mm_res_norm_tiles(kdim, d)
    n_k = kdim // tk
    kern = functools.partial(_mm_res_norm_kernel, n_k, emit_h, tm // EPILOGUE_ROWS)
    row_spec = pl.BlockSpec((tm, d), lambda m, k: (m, 0))
    if emit_h:
        out_shape = (jax.ShapeDtypeStruct((s, d), F32), jax.ShapeDtypeStruct((s, d), BF16))
        out_specs = (row_spec, row_spec)
        scratch = []
    else:
        out_shape = jax.ShapeDtypeStruct((s, d), F32)
        out_specs = row_spec
        scratch = [pltpu.VMEM((tm, d), F32)]
    return pl.pallas_call(
        kern,
        out_shape=out_shape,
        grid=(s // tm, n_k),
        in_specs=[pl.BlockSpec((tm, tk), lambda m, k: (m, k)),
                  pl.BlockSpec((None, tk, d), lambda m, k: (w_layer, k, 0)),
                  pl.BlockSpec((tm // n_k, d), lambda m, k: (m * n_k + k, 0)),
                  pl.BlockSpec((1, d), lambda m, k: (0, 0))],
        out_specs=out_specs,
        scratch_shapes=scratch,
        compiler_params=_params(2),
        name="matmul_residual_rmsnorm",
    )(a, w, resid, gain.reshape(1, d))


def _mlp_up_kernel(x_ref, w_ref, o_ref):
    a = jnp.maximum(_dot(x_ref[...], w_ref[...]), 0.0)
    o_ref[...] = (a * a).astype(o_ref.dtype)


def _mlp_up(x, w, *, w_layer, tm=1024, tn=2048):
    s, d = x.shape
    f = w.shape[2]
    return pl.pallas_call(
        _mlp_up_kernel,
        out_shape=jax.ShapeDtypeStruct((s, f), BF16),
        grid=(f // tn, s // tm),
        in_specs=[pl.BlockSpec((tm, d), lambda n, m: (m, 0)),
                  pl.BlockSpec((None, d, tn), lambda n, m: (w_layer, 0, n))],
        out_specs=pl.BlockSpec((tm, tn), lambda n, m: (m, n)),
        compiler_params=_params(2),
        name="mlp_up_relu2",
    )(x, w)


def _gmlp_kernel(n_pieces, hn_ref, w_ref, lng_ref, lnb_ref, wsp_ref, bsp_ref, o_ref):
    half = lng_ref.shape[1]
    piece = hn_ref.shape[0] // n_pieces
    row = lax.broadcasted_iota(jnp.int32, (GMLP_BLOCK, GMLP_BLOCK), 0)
    col = lax.broadcasted_iota(jnp.int32, (GMLP_BLOCK, GMLP_BLOCK), 1)
    causal = (col // CHUNK) <= (row // CHUNK)
    gd = half // N_GROUPS
    for p in range(n_pieces):
        z = _dot(hn_ref[p * piece:(p + 1) * piece, :], w_ref[...])
        z = 0.5 * z * (1.0 + lax.erf(z * (1.0 / math.sqrt(2.0))))
        u = z[:, 0:half]
        v = z[:, half:2 * half]
        mu = jnp.mean(v, axis=-1, keepdims=True)
        vc = v - mu
        var = jnp.mean(vc * vc, axis=-1, keepdims=True)
        vn = (vc * lax.rsqrt(var + NORM_EPS) * lng_ref[...] + lnb_ref[...]).astype(BF16)
        for g in range(N_GROUPS):
            wg = jnp.where(causal, wsp_ref[g], 0.0).astype(BF16)
            bias = bsp_ref[:, g:g + 1]
            cs = slice(g * gd, (g + 1) * gd)
            for r in range(piece // GMLP_BLOCK):
                rs = slice(r * GMLP_BLOCK, (r + 1) * GMLP_BLOCK)
                mixed = _dot(wg, vn[rs, cs]) + bias
                o_ref[p * piece + r * GMLP_BLOCK:p * piece + (r + 1) * GMLP_BLOCK, cs] = (
                    u[rs, cs] * mixed).astype(o_ref.dtype)


def _gmlp_mixer(hn, w_in, ln_gain, ln_bias, w_spatial, b_spatial_t, *, w_layer, tm=512, n_pieces=2):
    s, d = hn.shape
    two_half = w_in.shape[2]
    half = two_half // 2
    kern = functools.partial(_gmlp_kernel, n_pieces)
    const2 = lambda m: (0, 0)
    return pl.pallas_call(
        kern,
        out_shape=jax.ShapeDtypeStruct((s, half), BF16),
        grid=(s // tm,),
        in_specs=[pl.BlockSpec((tm, d), lambda m: (m, 0)),
                  pl.BlockSpec((None, d, two_half), lambda m: (w_layer, 0, 0), pipeline_mode=pl.Buffered(1)),
                  pl.BlockSpec((1, half), const2),
                  pl.BlockSpec((1, half), const2),
                  pl.BlockSpec((None, N_GROUPS, GMLP_BLOCK, GMLP_BLOCK), lambda m: (w_layer, 0, 0, 0)),
                  pl.BlockSpec((GMLP_BLOCK, N_GROUPS), const2)],
        out_specs=pl.BlockSpec((tm, half), lambda m: (m, 0)),
        compiler_params=_params(1),
        name="gmlp_mixer",
    )(hn, w_in, ln_gain.reshape(1, half), ln_bias.reshape(1, half), w_spatial, b_spatial_t)


def kernel(x, norm_mix, norm_mlp, final_norm, hgrn_w_in, hgrn_w_out, hgrn_g_norm, hgrn_lb_logits,
           gmlp_w_in, gmlp_w_out, gmlp_ln_gain, gmlp_ln_bias, gmlp_w_spatial, gmlp_b_spatial,
           mlp_w1, mlp_w2):
    b, s, d = x.shape
    depth = norm_mix.shape[0]
    h = x.reshape(b * s, d)
    assert b == 1, "sequence mixing assumes a single sequence"

    hn = _rmsnorm(h, norm_mix[0])
    side = (hgrn_w_out, gmlp_w_in, gmlp_w_out, mlp_w1, mlp_w2)
    mix, (hgrn_w_out_bf, gmlp_w_in_bf, gmlp_w_out_bf, mlp_w1_bf, mlp_w2_bf) = _hgrn_mixer(
        hn, hgrn_w_in, hgrn_lb_logits, hgrn_g_norm[0], w_layer=0, layer_idx=0, side_weights=side)
    out = None
    for i in range(depth):
        j = i // 2
        if i % 2 == 0:
            if i > 0:
                mix, _ = _hgrn_mixer(hn, hgrn_w_in, hgrn_lb_logits, hgrn_g_norm[j], w_layer=j, layer_idx=i)
            w_out = hgrn_w_out_bf
        else:
            mix = _gmlp_mixer(hn, gmlp_w_in_bf, gmlp_ln_gain[j], gmlp_ln_bias[j],
                              gmlp_w_spatial, gmlp_b_spatial[j].T, w_layer=j)
            w_out = gmlp_w_out_bf
        h, hn = _mm_res_norm(mix, w_out, h, norm_mlp[i], w_layer=j, emit_h=True)
        a = _mlp_up(hn, mlp_w1_bf, w_layer=i)
        if i + 1 < depth:
            h, hn = _mm_res_norm(a, mlp_w2_bf, h, norm_mix[i + 1], w_layer=i, emit_h=True)
        else:
            out = _mm_res_norm(a, mlp_w2_bf, h, final_norm, w_layer=i, emit_h=False)
    return out.reshape(b, s, d)
```

```python
import functools
import math

import jax
import jax.numpy as jnp
from jax import lax
from jax.experimental import pallas as pl
from jax.experimental.pallas import tpu as pltpu

F32 = jnp.float32
BF16 = jnp.bfloat16

NORM_EPS = 1e-6
CHUNK = 64
SUB = 16
TILE = 128
HEADS_PER_STEP = 2
PROJ_PIECES = ((1, 256), (3, 128), (5, 128))
HEAD = 128
GMLP_BLOCK = 128
N_GROUPS = 16
BF16_ROWS = 16
EPILOGUE_ROWS = 256

VMEM_LIMIT_BYTES = 56 * 1024 * 1024
RESIDENT_WEIGHT_BYTES = 8 * 1024 * 1024


def _dot(a, b):
    return jnp.dot(a, b, preferred_element_type=F32)


def _dot_nt(a, b):
    return lax.dot_general(a, b, (((1,), (1,)), ((), ())), preferred_element_type=F32)


def _sigmoid(x):
    return 1.0 / (1.0 + jnp.exp(-x))


def _rms_scale(h, gain):
    ms = jnp.mean(h * h, axis=-1, keepdims=True)
    return h * lax.rsqrt(ms + NORM_EPS) * gain


def _params(n_axes):
    return pltpu.CompilerParams(
        dimension_semantics=("arbitrary",) * n_axes, vmem_limit_bytes=VMEM_LIMIT_BYTES)


def _rmsnorm_kernel(x_ref, g_ref, o_ref):
    o_ref[...] = _rms_scale(x_ref[...], g_ref[...]).astype(o_ref.dtype)


def _rmsnorm(x, gain, *, tm=1024):
    s, d = x.shape
    return pl.pallas_call(
        _rmsnorm_kernel,
        out_shape=jax.ShapeDtypeStruct((s, d), BF16),
        grid=(s // tm,),
        in_specs=[pl.BlockSpec((tm, d), lambda m: (m, 0)),
                  pl.BlockSpec((1, d), lambda m: (0, 0))],
        out_specs=pl.BlockSpec((tm, d), lambda m: (m, 0)),
        compiler_params=_params(1),
        name="rmsnorm",
    )(x, gain.reshape(1, d))


def _split3_bf16(x):
    hi = x.astype(BF16)
    r1 = x - hi.astype(F32)
    mid = r1.astype(BF16)
    lo = (r1 - mid.astype(F32)).astype(BF16)
    return hi, mid, lo


def _row_blocks(src, blk, row_of_block):
    parts = []
    for j in range(src.shape[0] // blk):
        r = row_of_block(j)
        parts.append(jnp.broadcast_to(src[r:r + 1, :], (blk, src.shape[1])))
    return jnp.concatenate(parts, axis=0)


def _hgrn_project(layer_idx, rows, hn_ref, w_bf, lbl_ref, gn_ref, act_ref):
    lg = lbl_ref[...]
    lmax = jnp.max(lg, axis=0, keepdims=True)
    le = jnp.exp(lg - lmax)
    lb = jnp.sum(le[0:layer_idx + 1], axis=0, keepdims=True) / jnp.sum(le, axis=0, keepdims=True)

    width = act_ref.shape[2]
    proj = _dot(hn_ref[rows, :], w_bf[...])
    pq = proj[:, 0:width]
    pf = proj[:, width:2 * width]
    pg = proj[:, 3 * width:4 * width]
    forget = lb + (1.0 - lb) * _sigmoid(pf)
    act_ref[0, rows, :] = pq * _sigmoid(pq)
    act_ref[1, rows, :] = 1.0 - forget
    act_ref[2, rows, :] = proj[:, 2 * width:3 * width]
    act_ref[3, rows, :] = jnp.log(forget)
    act_ref[4, rows, :] = pg * _sigmoid(pg) * gn_ref[...]


def _hgrn_kernel(layer_idx, n_chunks, n_m, n_side, hn_ref, wq_ref, wf_ref, wi_ref, wg_ref, lbl_ref, gn_ref,
                 *refs):
    side_in, o_ref, side_out = refs[:n_side], refs[n_side], refs[n_side + 1:2 * n_side + 1]
    w_bf, st_ref, act_a, act_b = refs[2 * n_side + 1:]
    t = pl.program_id(0)
    n_blocks = pl.num_programs(0) - 1
    m_proj = lax.rem(jnp.minimum(t, n_blocks - 1), n_m)
    m_rec = lax.rem(jnp.maximum(t - 1, 0), n_m)

    @pl.when(t == 0)
    def _():
        act_b[...] = jnp.zeros_like(act_b)

    @pl.when(m_proj == 0)
    def _():
        width = act_a.shape[2]
        for p, r in enumerate((wq_ref, wf_ref, wi_ref, wg_ref)):
            w_bf[:, p * width:(p + 1) * width] = r[...].astype(BF16)

    @pl.when(m_rec == 0)
    def _():
        st_ref[...] = jnp.zeros_like(st_ref)

    for parity, (act_new, act_old) in enumerate(((act_a, act_b), (act_b, act_a))):
        @pl.when(lax.rem(t, 2) == parity)
        def _():
            heads = [_hgrn_recurrence(n_chunks, hl, act_old, o_ref, st_ref) for hl in range(HEADS_PER_STEP)]
            stage = 0
            row0 = 0
            for after, piece in PROJ_PIECES:
                while stage < after:
                    for stages in heads:
                        next(stages, None)
                    if stage < n_side:
                        side_out[stage][...] = side_in[stage][...].astype(BF16)
                    stage += 1
                _hgrn_project(layer_idx, slice(row0, row0 + piece), hn_ref, w_bf, lbl_ref, gn_ref, act_new)
                row0 += piece
            assert row0 == n_chunks * CHUNK
            for stages in heads:
                for _ in stages:
                    pass
            for i in range(stage, n_side):
                side_out[i][...] = side_in[i][...].astype(BF16)


def _hgrn_recurrence(n_chunks, hl, act_ref, o_ref, st_ref):
    lanes = slice(hl * HEAD, (hl + 1) * HEAD)
    q, k, v, logf, gate = (act_ref[i, :, lanes] for i in range(5))
    tm = n_chunks * CHUNK
    n_tiles = tm // TILE

    def chunk(x, c):
        return x[c * CHUNK:(c + 1) * CHUNK]

    def chunks_to_lanes(x):
        return jnp.concatenate([chunk(x, c) for c in range(n_chunks)], axis=1)

    row_c = lax.broadcasted_iota(jnp.int32, (CHUNK, CHUNK), 0)
    col_c = lax.broadcasted_iota(jnp.int32, (CHUNK, CHUNK), 1)
    tril = (col_c <= row_c).astype(BF16)
    hi, mid, lo = _split3_bf16(logf)
    cs = _dot(tril, jnp.concatenate([chunks_to_lanes(hi), chunks_to_lanes(mid), chunks_to_lanes(lo)], axis=1))
    wl = n_chunks * HEAD
    bl = cs[:, 0:wl] + cs[:, wl:2 * wl] + cs[:, 2 * wl:3 * wl]
    b_parts = []
    offset = None
    for c in range(n_chunks):
        bc = bl[:, c * HEAD:(c + 1) * HEAD]
        if offset is not None:
            bc = bc + offset
        b_parts.append(bc)
        offset = bc[CHUNK - 1:CHUNK, :]
    b = jnp.concatenate(b_parts, axis=0)
    b_last = offset
    yield

    v_bf = v.astype(BF16)
    a_tiles = [[None] * n_tiles for _ in range(n_tiles)]

    half = tm // 2
    while half >= TILE:
        blk = 2 * half
        for j in range(tm // blk):
            ref_row = j * blk + half - 1
            t_rows = slice(j * blk + half, (j + 1) * blk)
            s_rows = slice(j * blk, j * blk + half)
            b_ref = b[ref_row:ref_row + 1, :]
            qz = (q[t_rows] * jnp.exp(b[t_rows] - b_ref)).astype(BF16)
            kz = (k[s_rows] * jnp.exp(b_ref - b[s_rows])).astype(BF16)
            p = _dot_nt(qz, kz).astype(BF16)
            for ti in range(half // TILE):
                for si in range(half // TILE):
                    a_tiles[(j * blk + half) // TILE + ti][(j * blk) // TILE + si] = (
                        p[ti * TILE:(ti + 1) * TILE, si * TILE:(si + 1) * TILE])
        half //= 2
    yield

    row = lax.broadcasted_iota(jnp.int32, (TILE, TILE), 0)
    col = lax.broadcasted_iota(jnp.int32, (TILE, TILE), 1)
    e_sub = b - _row_blocks(b, SUB, lambda i: i * SUB + SUB // 2 - 1)
    q_parts = [(q * jnp.exp(e_sub)).astype(BF16)]
    k_parts = [(k * jnp.exp(-e_sub)).astype(BF16)]
    masks = [(row // SUB == col // SUB) & (col <= row)]
    half = SUB
    while half < TILE:
        blk = 2 * half
        z = jnp.exp(-jnp.abs(b - _row_blocks(b, blk, lambda j, blk=blk, half=half: j * blk + half - 1)))
        q_parts.append((q * z).astype(BF16))
        k_parts.append((k * z).astype(BF16))
        masks.append((row // blk == col // blk) & (row % blk >= half) & (col % blk < half))
        half = blk
    yield
    for i in range(n_tiles):
        rows = slice(i * TILE, (i + 1) * TILE)
        scores = jnp.zeros((TILE, TILE), F32)
        for qp, kp, msk in zip(q_parts, k_parts, masks):
            scores = jnp.where(msk, _dot_nt(qp[rows], kp[rows]), scores)
        a_tiles[i][i] = scores.astype(BF16)
    yield

    st = st_ref[hl]
    o_state = _dot_nt((q * jnp.exp(b)).astype(BF16), st.astype(BF16))
    outs = []
    for i in range(n_tiles):
        a_row = jnp.concatenate(a_tiles[i][0:i + 1], axis=1)
        outs.append(_dot(a_row, v_bf[0:(i + 1) * TILE]))
    o = jnp.concatenate(outs, axis=0) + o_state

    k_hat = (k * jnp.exp(b_last - b)).astype(BF16)
    st_ref[hl] = st * jnp.exp(b_last) + _dot(v.T.astype(BF16), k_hat)

    ms = jnp.mean(o * o, axis=-1, keepdims=True)
    o_ref[:, lanes] = (o * lax.rsqrt(ms + NORM_EPS) * gate).astype(o_ref.dtype)


def _hgrn_mixer(hn, w_in, lb_logits, g_norm, *, w_layer, layer_idx, side_weights=(), tm=512):
    s, d = hn.shape
    n_heads = w_in.shape[2] // (4 * HEAD)
    n_lb = lb_logits.shape[0]
    n_m = s // tm
    width = HEADS_PER_STEP * HEAD
    n_groups = n_heads // HEADS_PER_STEP
    n_blocks = n_groups * n_m
    kern = functools.partial(_hgrn_kernel, layer_idx, tm // CHUNK, n_m, len(side_weights))

    def proj_block(t):
        tp = jnp.minimum(t, n_blocks - 1)
        return tp // n_m, lax.rem(tp, n_m)

    def rec_block(t):
        tr = jnp.maximum(t - 1, 0)
        return tr // n_m, lax.rem(tr, n_m)

    def w_spec(part):
        return pl.BlockSpec((None, d, width), lambda t: (w_layer, 0, part * n_groups + proj_block(t)[0]))

    side_2d, side_in_specs, side_out_specs, side_shapes = [], [], [], []
    for sw in side_weights:
        cols = sw.shape[-1]
        rows = sw.size // cols
        blk = rows // n_blocks
        assert blk * n_blocks == rows and blk % BF16_ROWS == 0, (sw.shape, n_blocks)
        side_2d.append(sw.reshape(rows, cols))
        for specs in (side_in_specs, side_out_specs):
            specs.append(pl.BlockSpec((blk, cols), lambda t: (jnp.minimum(t, n_blocks - 1), 0)))
        side_shapes.append(jax.ShapeDtypeStruct((rows, cols), BF16))

    act = pltpu.VMEM((5, tm, width), F32)
    outs = pl.pallas_call(
        kern,
        out_shape=[jax.ShapeDtypeStruct((s, n_heads * HEAD), BF16)] + side_shapes,
        grid=(n_blocks + 1,),
        in_specs=[pl.BlockSpec((tm, d), lambda t: (proj_block(t)[1], 0)),
                  w_spec(0), w_spec(1), w_spec(2), w_spec(3),
                  pl.BlockSpec((n_lb, width), lambda t: (0, proj_block(t)[0])),
                  pl.BlockSpec((1, width), lambda t: (0, 0))] + side_in_specs,
        out_specs=[pl.BlockSpec((tm, width), lambda t: (rec_block(t)[1], rec_block(t)[0]))] + side_out_specs,
        scratch_shapes=[pltpu.VMEM((d, 4 * width), BF16), pltpu.VMEM((HEADS_PER_STEP, HEAD, HEAD), F32),
                        act, act],
        compiler_params=_params(1),
        name="hgrn_mixer",
    )(hn, w_in, w_in, w_in, w_in, lb_logits, jnp.tile(g_norm.reshape(1, HEAD), (1, HEADS_PER_STEP)), *side_2d)
    return outs[0], [o.reshape(sw.shape) for o, sw in zip(outs[1:], side_weights)]


def _mm_res_norm_kernel(n_k, emit_h, n_pieces, a_ref, w_ref, r_ref, g_ref, *out_refs):
    if emit_h:
        h_ref, hn_ref = out_refs
        acc_ref = h_ref
    else:
        hn_ref, acc_ref = out_refs
    kk = pl.program_id(1)
    piece = a_ref.shape[0] // n_pieces
    res_rows = r_ref.shape[0]

    def step(first, last):
        if not last:
            part = _dot(a_ref[...], w_ref[...])
            acc_ref[...] = part if first else acc_ref[...] + part
            rows = pl.ds(pl.multiple_of(kk * res_rows, res_rows), res_rows)
            acc_ref[rows, :] = acc_ref[rows, :] + r_ref[...]
            return
        for p in range(n_pieces):
            rows = slice(p * piece, (p + 1) * piece)
            part = _dot(a_ref[rows, :], w_ref[...])
            h = part if first else acc_ref[rows, :] + part
            lo = max(rows.start, (n_k - 1) * res_rows)
            if lo < rows.stop:
                res = r_ref[lo - (n_k - 1) * res_rows:rows.stop - (n_k - 1) * res_rows, :]
                if lo > rows.start:
                    res = jnp.concatenate([jnp.zeros((lo - rows.start, res.shape[1]), F32), res], axis=0)
                h = h + res
            if emit_h:
                h_ref[rows, :] = h
            hn_ref[rows, :] = _rms_scale(h, g_ref[...]).astype(hn_ref.dtype)

    if n_k == 1:
        step(True, True)
    else:
        pl.when(kk == 0)(lambda: step(True, False))
        pl.when((kk > 0) & (kk < n_k - 1))(lambda: step(False, False))
        pl.when(kk == n_k - 1)(lambda: step(False, True))


def _mm_res_norm_tiles(kdim, d):
    if kdim * d * jnp.dtype(BF16).itemsize <= RESIDENT_WEIGHT_BYTES:
        return 512, kdim
    return 1024, 2048


def _mm_res_norm(a, w, resid, gain, *, w_layer, emit_h):
    s, kdim = a.shape
    d = w.shape[2]
    tm, tk = _mm_res_norm_tiles(kdim, d)
    n_k = kdim // tk
    kern = functools.partial(_mm_res_norm_kernel, n_k, emit_h, tm // EPILOGUE_ROWS)
    row_spec = pl.BlockSpec((tm, d), lambda m, k: (m, 0))
    if emit_h:
        out_shape = (jax.ShapeDtypeStruct((s, d), F32), jax.ShapeDtypeStruct((s, d), BF16))
        out_specs = (row_spec, row_spec)
        scratch = []
    else:
        out_shape = jax.ShapeDtypeStruct((s, d), F32)
        out_specs = row_spec
        scratch = [pltpu.VMEM((tm, d), F32)]
    return pl.pallas_call(
        kern,
        out_shape=out_shape,
        grid=(s // tm, n_k),
        in_specs=[pl.BlockSpec((tm, tk), lambda m, k: (m, k)),
                  pl.BlockSpec((None, tk, d), lambda m, k: (w_layer, k, 0)),
                  pl.BlockSpec((tm // n_k, d), lambda m, k: (m * n_k + k, 0)),
                  pl.BlockSpec((1, d), lambda m, k: (0, 0))],
        out_specs=out_specs,
        scratch_shapes=scratch,
        compiler_params=_params(2),
        name="matmul_residual_rmsnorm",
    )(a, w, resid, gain.reshape(1, d))


def _mlp_up_kernel(x_ref, w_ref, o_ref):
    a = jnp.maximum(_dot(x_ref[...], w_ref[...]), 0.0)
    o_ref[...] = (a * a).astype(o_ref.dtype)


def _mlp_up(x, w, *, w_layer, tm=1024, tn=2048):
    s, d = x.shape
    f = w.shape[2]
    return pl.pallas_call(
        _mlp_up_kernel,
        out_shape=jax.ShapeDtypeStruct((s, f), BF16),
        grid=(f // tn, s // tm),
        in_specs=[pl.BlockSpec((tm, d), lambda n, m: (m, 0)),
                  pl.BlockSpec((None, d, tn), lambda n, m: (w_layer, 0, n))],
        out_specs=pl.BlockSpec((tm, tn), lambda n, m: (m, n)),
        compiler_params=_params(2),
        name="mlp_up_relu2",
    )(x, w)


def _gmlp_kernel(n_pieces, hn_ref, w_ref, lng_ref, lnb_ref, wsp_ref, bsp_ref, o_ref):
    half = lng_ref.shape[1]
    piece = hn_ref.shape[0] // n_pieces
    row = lax.broadcasted_iota(jnp.int32, (GMLP_BLOCK, GMLP_BLOCK), 0)
    col = lax.broadcasted_iota(jnp.int32, (GMLP_BLOCK, GMLP_BLOCK), 1)
    causal = (col // CHUNK) <= (row // CHUNK)
    gd = half // N_GROUPS
    for p in range(n_pieces):
        x = hn_ref[p * piece:(p + 1) * piece, :]
        v = _dot(x, w_ref[:, half:2 * half])
        v = 0.5 * v * (1.0 + lax.erf(v * (1.0 / math.sqrt(2.0))))
        mu = jnp.mean(v, axis=-1, keepdims=True)
        vc = v - mu
        var = jnp.mean(vc * vc, axis=-1, keepdims=True)
        vn = (vc * lax.rsqrt(var + NORM_EPS) * lng_ref[...] + lnb_ref[...]).astype(BF16)
        u = _dot(x, w_ref[:, 0:half])
        u = 0.5 * u * (1.0 + lax.erf(u * (1.0 / math.sqrt(2.0))))
        for g in range(N_GROUPS):
            wg = jnp.where(causal, wsp_ref[g], 0.0).astype(BF16)
            bias = bsp_ref[:, g:g + 1]
            cs = slice(g * gd, (g + 1) * gd)
            for r in range(piece // GMLP_BLOCK):
                rs = slice(r * GMLP_BLOCK, (r + 1) * GMLP_BLOCK)
                mixed = _dot(wg, vn[rs, cs]) + bias
                o_ref[p * piece + r * GMLP_BLOCK:p * piece + (r + 1) * GMLP_BLOCK, cs] = (
                    u[rs, cs] * mixed).astype(o_ref.dtype)


def _gmlp_mixer(hn, w_in, ln_gain, ln_bias, w_spatial, b_spatial_t, *, w_layer, tm=512, n_pieces=2):
    s, d = hn.shape
    two_half = w_in.shape[2]
    half = two_half // 2
    kern = functools.partial(_gmlp_kernel, n_pieces)
    const2 = lambda m: (0, 0)
    return pl.pallas_call(
        kern,
        out_shape=jax.ShapeDtypeStruct((s, half), BF16),
        grid=(s // tm,),
        in_specs=[pl.BlockSpec((tm, d), lambda m: (m, 0)),
                  pl.BlockSpec((None, d, two_half), lambda m: (w_layer, 0, 0), pipeline_mode=pl.Buffered(1)),
                  pl.BlockSpec((1, half), const2),
                  pl.BlockSpec((1, half), const2),
                  pl.BlockSpec((None, N_GROUPS, GMLP_BLOCK, GMLP_BLOCK), lambda m: (w_layer, 0, 0, 0)),
                  pl.BlockSpec((GMLP_BLOCK, N_GROUPS), const2)],
        out_specs=pl.BlockSpec((tm, half), lambda m: (m, 0)),
        compiler_params=_params(1),
        name="gmlp_mixer",
    )(hn, w_in, ln_gain.reshape(1, half), ln_bias.reshape(1, half), w_spatial, b_spatial_t)


def kernel(x, norm_mix, norm_mlp, final_norm, hgrn_w_in, hgrn_w_out, hgrn_g_norm, hgrn_lb_logits,
           gmlp_w_in, gmlp_w_out, gmlp_ln_gain, gmlp_ln_bias, gmlp_w_spatial, gmlp_b_spatial,
           mlp_w1, mlp_w2):
    b, s, d = x.shape
    depth = norm_mix.shape[0]
    h = x.reshape(b * s, d)
    assert b == 1, "sequence mixing assumes a single sequence"

    hn = _rmsnorm(h, norm_mix[0])
    side = (hgrn_w_out, gmlp_w_in, gmlp_w_out, mlp_w1, mlp_w2)
    mix, (hgrn_w_out_bf, gmlp_w_in_bf, gmlp_w_out_bf, mlp_w1_bf, mlp_w2_bf) = _hgrn_mixer(
        hn, hgrn_w_in, hgrn_lb_logits, hgrn_g_norm[0], w_layer=0, layer_idx=0, side_weights=side)
    out = None
    for i in range(depth):
        j = i // 2
        if i % 2 == 0:
            if i > 0:
                mix, _ = _hgrn_mixer(hn, hgrn_w_in, hgrn_lb_logits, hgrn_g_norm[j], w_layer=j, layer_idx=i)
            w_out = hgrn_w_out_bf
        else:
            mix = _gmlp_mixer(hn, gmlp_w_in_bf, gmlp_ln_gain[j], gmlp_ln_bias[j],
                              gmlp_w_spatial, gmlp_b_spatial[j].T, w_layer=j)
            w_out = gmlp_w_out_bf
        h, hn = _mm_res_norm(mix, w_out, h, norm_mlp[i], w_layer=j, emit_h=True)
        a = _mlp_up(hn, mlp_w1_bf, w_layer=i)
        if i + 1 < depth:
            h, hn = _mm_res_norm(a, mlp_w2_bf, h, norm_mix[i + 1], w_layer=i, emit_h=True)
        else:
            out = _mm_res_norm(a, mlp_w2_bf, h, final_norm, w_layer=i, emit_h=False)
    return out.reshape(b, s, d)
```
